```python
import math
import jax, jax.numpy as jnp
from jax import lax
import numpy as np

D_MODEL = 2048
BATCH = 4
SEQ = 8192
DEPTH = 4
DEC_BATCH = 32
DEC_SEQ = 64
PAST_LEN = 1024

CHUNK = 64
Q_BLOCK = 128
N_GROUPS = 4
HEADS_PER_GROUP = 4
HEAD_DIM = D_MODEL // (N_GROUPS * HEADS_PER_GROUP)
GROUP_WIDTH = HEADS_PER_GROUP * HEAD_DIM
MIX_WIDTH = N_GROUPS * GROUP_WIDTH
QKV_WIDTH = 3 * MIX_WIDTH
IN_WIDTH = QKV_WIDTH + HEADS_PER_GROUP
D_FF = 4 * D_MODEL
BAND_CHUNKS = 8
BAND_PAST = BAND_CHUNKS * CHUNK
BAND_KEYS = BAND_PAST + CHUNK
REL_CLIP = 128
DIFF_HALF = HEAD_DIM // 2
ROPE_THETA = 10000.0
FORGET_BIAS_INIT = 1.0
EPS = 1e-6
NEG_INF = -1e30
F32 = jnp.float32

kernel_name = 'hybrid_streaming_encoder_step'


def rmsnorm(x, g):
    xf = x.astype(F32)
    y = xf * lax.rsqrt(jnp.mean(xf * xf, axis=-1, keepdims=True) + EPS)
    return (y * g.astype(F32)).astype(x.dtype)


def mlp(x, g, w_up_l, w_down_l):
    u = jax.nn.relu(rmsnorm(x, g) @ w_up_l)
    return (u * u) @ w_down_l


def project(h, w_in_l, fox_b_l):
    bsz, t = h.shape[:2]
    p = h @ w_in_l
    qkv = p[..., :QKV_WIDTH].reshape(bsz, t, N_GROUPS, 3, HEADS_PER_GROUP, HEAD_DIM)
    logf = jax.nn.log_sigmoid(p[..., QKV_WIDTH:].astype(F32) + fox_b_l.astype(F32))
    return qkv, logf


def rope_pairs(x, pos):
    half = DIFF_HALF // 2
    inv = ROPE_THETA ** (-jnp.arange(half, dtype=F32) / half)
    ang = pos.astype(F32)[:, None] * inv[None, :]
    cos = jnp.cos(ang)[None, :, None, None, :]
    sin = jnp.sin(ang)[None, :, None, None, :]
    xf = x.astype(F32)
    x1, x2 = xf[..., :half], xf[..., half:]
    return jnp.concatenate([x1 * cos - x2 * sin, x2 * cos + x1 * sin], axis=-1).astype(x.dtype)


def diff_lambda_value(lam_l, lam_init):
    lf = lam_l.astype(F32)
    return jnp.exp(jnp.sum(lf[0] * lf[1])) - jnp.exp(jnp.sum(lf[2] * lf[3])) + lam_init


def fox_attend(q, k, v, fq, fk, q_pos, k_pos):
    s = jnp.einsum('bqhd,bkhd->bhqk', q, k).astype(F32) * HEAD_DIM ** -0.5
    s = s + jnp.swapaxes(fq, 1, 2)[:, :, :, None] - jnp.swapaxes(fk, 1, 2)[:, :, None, :]
    mask = k_pos[None, :] <= q_pos[:, None]
    p = jax.nn.softmax(jnp.where(mask, s, NEG_INF), axis=-1)
    return jnp.einsum('bhqk,bkhd->bqhd', p.astype(v.dtype), v)


def band_attend(q, k, v, q_pos, k_pos, rel_l):
    s = jnp.einsum('bqhd,bkhd->bhqk', q, k).astype(F32) * HEAD_DIM ** -0.5
    rel = jnp.clip(q_pos[:, None] - k_pos[None, :], -REL_CLIP, REL_CLIP) + REL_CLIP
    s = s + rel_l.astype(F32)[:, rel]
    qc, kc = q_pos[:, None] // CHUNK, k_pos[None, :] // CHUNK
    mask = (kc <= qc) & (kc >= qc - BAND_CHUNKS) & (k_pos[None, :] >= 0)
    p = jax.nn.softmax(jnp.where(mask, s, NEG_INF), axis=-1)
    return jnp.einsum('bhqk,bkhd->bqhd', p.astype(v.dtype), v)


def diff_attend(q, k, v, q_pos, k_pos, lam, lam_init, g):
    s = jnp.einsum('bqhcd,bkhcd->bhcqk', q, k).astype(F32) * DIFF_HALF ** -0.5
    mask = (k_pos[None, :] // CHUNK) <= (q_pos[:, None] // CHUNK)
    p = jax.nn.softmax(jnp.where(mask, s, NEG_INF), axis=-1)
    w = p[:, :, 0] - lam * p[:, :, 1]
    o = jnp.einsum('bhqk,bkhd->bqhd', w.astype(v.dtype), v)
    return rmsnorm(o, g) * (1.0 - lam_init)


def sb_attend(q, k, v, q_pos, k_pos):
    z = jnp.einsum('bqhd,bkhd->bhqk', q, k).astype(F32) * HEAD_DIM ** -0.5
    mask = k_pos[None, :] < q_pos[:, None]
    log_one_minus = jnp.where(mask, jax.nn.log_sigmoid(-z), 0.0)
    suffix = lax.cumsum(log_one_minus, axis=3, reverse=True) - log_one_minus
    a = jnp.where(mask, jnp.exp(jax.nn.log_sigmoid(z) + suffix), 0.0)
    return jnp.einsum('bhqk,bkhd->bqhd', a.astype(v.dtype), v)


def to_blocks(a, size):
    b, t = a.shape[:2]
    return jnp.moveaxis(a.reshape(b, t // size, size, *a.shape[2:]), 1, 0)


def from_blocks(a):
    nb, b, size = a.shape[:3]
    return jnp.moveaxis(a, 0, 1).reshape(b, nb * size, *a.shape[3:])


def mix_prompt(h, w_in_l, fox_b_l, rel_l, lam_l, dng_l, lam_init):
    bsz, t = h.shape[:2]
    hh = HEADS_PER_GROUP
    qkv, logf = project(h, w_in_l, fox_b_l)
    pos = jnp.arange(t)
    pos_blocks = pos.reshape(t // Q_BLOCK, Q_BLOCK)
    lam = diff_lambda_value(lam_l, lam_init)

    qa, ka, va = qkv[:, :, 0, 0], qkv[:, :, 0, 1], qkv[:, :, 0, 2]
    cum_f = jnp.cumsum(logf, axis=1)
    oa = from_blocks(lax.map(
        lambda a: fox_attend(a[0], ka, va, a[1], cum_f, a[2], pos),
        (to_blocks(qa, Q_BLOCK), to_blocks(cum_f, Q_BLOCK), pos_blocks)))

    qb, kb, vb = qkv[:, :, 1, 0], qkv[:, :, 1, 1], qkv[:, :, 1, 2]
    pad = ((0, 0), (BAND_PAST, 0), (0, 0), (0, 0))
    kb_pad, vb_pad = jnp.pad(kb, pad), jnp.pad(vb, pad)

    def band_chunk(a):
        qch, start = a
        kk = lax.dynamic_slice_in_dim(kb_pad, start, BAND_KEYS, axis=1)
        vv = lax.dynamic_slice_in_dim(vb_pad, start, BAND_KEYS, axis=1)
        return band_attend(qch, kk, vv, start + jnp.arange(CHUNK),
                           start - BAND_PAST + jnp.arange(BAND_KEYS), rel_l)

    ob = from_blocks(lax.map(band_chunk, (to_blocks(qb, CHUNK), jnp.arange(0, t, CHUNK))))

    qc = rope_pairs(qkv[:, :, 2, 0].reshape(bsz, t, hh, 2, DIFF_HALF), pos)
    kc = rope_pairs(qkv[:, :, 2, 1].reshape(bsz, t, hh, 2, DIFF_HALF), pos)
    vc = qkv[:, :, 2, 2]
    oc = from_blocks(lax.map(
        lambda a: diff_attend(a[0], kc, vc, a[1], pos, lam, lam_init, dng_l),
        (to_blocks(qc, Q_BLOCK), pos_blocks)))

    qd, kd, vd = qkv[:, :, 3, 0], qkv[:, :, 3, 1], qkv[:, :, 3, 2]
    od = from_blocks(lax.map(
        lambda a: sb_attend(a[0], kd, vd, a[1], pos),
        (to_blocks(qd, Q_BLOCK), pos_blocks)))

    o = jnp.concatenate([oa, ob, oc, od], axis=2).reshape(bsz, t, MIX_WIDTH)
    lbp = min(BAND_PAST, t)
    states = (ka, va, logf, kb[:, t - lbp:], vb[:, t - lbp:],
              kc.reshape(bsz, t, hh, HEAD_DIM), vc, kd, vd)
    return o, states


def mix_sample(h, ca_k, ca_v, ca_f, cb_k, cb_v, cc_k, cc_v, cd_k, cd_v,
               w_in_l, fox_b_l, rel_l, lam_l, dng_l, lam_init):
    bsz, t = h.shape[:2]
    hh = HEADS_PER_GROUP
    past = ca_k.shape[1]
    qkv, logf = project(h, w_in_l, fox_b_l)
    q_pos = past + jnp.arange(t)
    k_pos = jnp.arange(past + t)
    lam = diff_lambda_value(lam_l, lam_init)

    qa, ka, va = qkv[:, :, 0, 0], qkv[:, :, 0, 1], qkv[:, :, 0, 2]
    ka_all = jnp.concatenate([ca_k, ka], axis=1)
    va_all = jnp.concatenate([ca_v, va], axis=1)
    cum_f = jnp.cumsum(jnp.concatenate([ca_f.astype(F32), logf], axis=1), axis=1)
    oa = fox_attend(qa, ka_all, va_all, cum_f[:, past:], cum_f, q_pos, k_pos)

    lb = cb_k.shape[1]
    qb, kb, vb = qkv[:, :, 1, 0], qkv[:, :, 1, 1], qkv[:, :, 1, 2]
    kb_all = jnp.concatenate([cb_k, kb], axis=1)
    vb_all = jnp.concatenate([cb_v, vb], axis=1)
    ob = band_attend(qb, kb_all, vb_all, q_pos, past - lb + jnp.arange(lb + t), rel_l)

    qc = rope_pairs(qkv[:, :, 2, 0].reshape(bsz, t, hh, 2, DIFF_HALF), q_pos)
    kc = rope_pairs(qkv[:, :, 2, 1].reshape(bsz, t, hh, 2, DIFF_HALF), q_pos)
    vc = qkv[:, :, 2, 2]
    kc_all = jnp.concatenate([cc_k.reshape(bsz, past, hh, 2, DIFF_HALF), kc], axis=1)
    vc_all = jnp.concatenate([cc_v, vc], axis=1)
    oc = diff_attend(qc, kc_all, vc_all, q_pos, k_pos, lam, lam_init, dng_l)

    qd, kd, vd = qkv[:, :, 3, 0], qkv[:, :, 3, 1], qkv[:, :, 3, 2]
    kd_all = jnp.concatenate([cd_k, kd], axis=1)
    vd_all = jnp.concatenate([cd_v, vd], axis=1)
    od = sb_attend(qd, kd_all, vd_all, q_pos, k_pos)

    o = jnp.concatenate([oa, ob, oc, od], axis=2).reshape(bsz, t, MIX_WIDTH)
    states = (ka, va, logf, kb_all[:, t:], vb_all[:, t:],
              kc.reshape(bsz, t, hh, HEAD_DIM), vc, kd, vd)
    return o, states


def setup_inputs(seed: int = 0) -> dict:
    key = jax.random.key(seed)
    ks = jax.random.split(key, 24)
    hh = HEADS_PER_GROUP
    lb = min(BAND_PAST, PAST_LEN)

    def nrm(k, shape, scale=1.0):
        return scale * jax.random.normal(k, shape, F32)

    kv_shape = (DEPTH, DEC_BATCH, PAST_LEN, hh, HEAD_DIM)
    band_shape = (DEPTH, DEC_BATCH, lb, hh, HEAD_DIM)
    return {
        'x_prompt': nrm(ks[0], (BATCH, SEQ, D_MODEL)),
        'x_sample': nrm(ks[1], (DEC_BATCH, DEC_SEQ, D_MODEL)),
        'cache_a_k': nrm(ks[2], kv_shape),
        'cache_a_v': nrm(ks[3], kv_shape),
        'cache_a_logf': jax.nn.log_sigmoid(FORGET_BIAS_INIT + nrm(ks[4], (DEPTH, DEC_BATCH, PAST_LEN, hh))),
        'cache_b_k': nrm(ks[5], band_shape),
        'cache_b_v': nrm(ks[6], band_shape),
        'cache_c_k': nrm(ks[7], kv_shape),
        'cache_c_v': nrm(ks[8], kv_shape),
        'cache_d_k': nrm(ks[9], kv_shape),
        'cache_d_v': nrm(ks[10], kv_shape),
        'attn_norm_g': 1.0 + nrm(ks[11], (DEPTH, D_MODEL), 0.05),
        'w_in': nrm(ks[12], (DEPTH, D_MODEL, IN_WIDTH), D_MODEL ** -0.5),
        'fox_b': FORGET_BIAS_INIT + nrm(ks[13], (DEPTH, hh), 0.1),
        'band_rel_bias': nrm(ks[14], (DEPTH, hh, 2 * REL_CLIP + 1), 0.5),
        'diff_lambda': nrm(ks[15], (DEPTH, 4, DIFF_HALF), 0.1),
        'diff_norm_g': 1.0 + nrm(ks[16], (DEPTH, HEAD_DIM), 0.05),
        'w_out': nrm(ks[17], (DEPTH, MIX_WIDTH, D_MODEL), MIX_WIDTH ** -0.5),
        'mlp_norm_g': 1.0 + nrm(ks[18], (DEPTH, D_MODEL), 0.05),
        'w_up': nrm(ks[19], (DEPTH, D_MODEL, D_FF), D_MODEL ** -0.5),
        'w_down': nrm(ks[20], (DEPTH, D_FF, D_MODEL), D_FF ** -0.5),
        'final_norm_g': 1.0 + nrm(ks[21], (D_MODEL,), 0.05),
    }


def reference(x_prompt, x_sample, cache_a_k, cache_a_v, cache_a_logf, cache_b_k, cache_b_v,
              cache_c_k, cache_c_v, cache_d_k, cache_d_v, attn_norm_g, w_in, fox_b,
              band_rel_bias, diff_lambda, diff_norm_g, w_out, mlp_norm_g, w_up, w_down,
              final_norm_g):
    xp, xs = x_prompt, x_sample
    p_states, s_states = [], []
    for l in range(DEPTH):
        lam_init = 0.8 - 0.6 * math.exp(-0.3 * l)
        shared = (w_in[l], fox_b[l], band_rel_bias[l], diff_lambda[l], diff_norm_g[l], lam_init)

        o_p, st_p = mix_prompt(rmsnorm(xp, attn_norm_g[l]), *shared)
        xp = xp + o_p @ w_out[l]
        xp = xp + mlp(xp, mlp_norm_g[l], w_up[l], w_down[l])

        o_s, st_s = mix_sample(rmsnorm(xs, attn_norm_g[l]),
                               cache_a_k[l], cache_a_v[l], cache_a_logf[l], cache_b_k[l], cache_b_v[l],
                               cache_c_k[l], cache_c_v[l], cache_d_k[l], cache_d_v[l], *shared)
        xs = xs + o_s @ w_out[l]
        xs = xs + mlp(xs, mlp_norm_g[l], w_up[l], w_down[l])

        p_states.append(st_p)
        s_states.append(st_s)

    pa_k, pa_v, pa_logf, pb_k, pb_v, pc_k, pc_v, pd_k, pd_v = [jnp.stack(s) for s in zip(*p_states)]
    sa_k, sa_v, sa_logf, sb_k, sb_v, sc_k, sc_v, sd_k, sd_v = [jnp.stack(s) for s in zip(*s_states)]
    y_prompt = rmsnorm(xp, final_norm_g)
    y_sample = rmsnorm(xs, final_norm_g)
    return (y_prompt, y_sample, pa_k, pa_v, pa_logf, pb_k, pb_v, pc_k, pc_v, pd_k, pd_v,
            sa_k, sa_v, sa_logf, sb_k, sb_v, sc_k, sc_v, sd_k, sd_v)
```

```python
import functools

import numpy as np
import jax
import jax.numpy as jnp
from jax import lax
from jax.experimental import pallas as pl
from jax.experimental.pallas import tpu as pltpu

F32 = jnp.float32
BF16 = jnp.bfloat16

LANES = 128
HEAD_DIM = 128
HEADS = 4
GROUP_W = HEADS * HEAD_DIM
N_GROUPS = 4
QKV_W = 3 * N_GROUPS * GROUP_W
CHUNK = 64
BAND_CHUNKS = 8
BAND_PAST = BAND_CHUNKS * CHUNK
REL_CLIP = 128
DIFF_HALF = HEAD_DIM // 2
ROPE_HALF = DIFF_HALF // 2
ROPE_THETA = 10000.0
EPS = 1e-6
NEG_INF = -1e30
VMEM_LIMIT = 48 * 1024 * 1024

TILE_CQ = 2 * 3 + 0
TILE_CK = 2 * 3 + 1


def _cparams(*sem):
    return pltpu.CompilerParams(dimension_semantics=sem, vmem_limit_bytes=VMEM_LIMIT)


def _dot(a, b):
    return jnp.dot(a, b, preferred_element_type=F32)


def _dot_nt(a, b):
    return lax.dot_general(a, b, (((1,), (1,)), ((), ())), preferred_element_type=F32)


def _rmsnorm_rows(x, g):
    return x * lax.rsqrt(jnp.mean(x * x, axis=-1, keepdims=True) + EPS) * g


def _log_sigmoid(z):
    return jnp.minimum(z, 0.0) - jnp.log1p(jnp.exp(-jnp.abs(z)))


def _rope_tile(y, cos, sin):
    lane = lax.broadcasted_iota(jnp.int32, (1, LANES), 1)
    first = (lane % DIFF_HALF) < ROPE_HALF
    outs = []
    for h in range(HEADS):
        x = y[:, h * LANES:(h + 1) * LANES]
        partner = jnp.where(first, pltpu.roll(x, LANES - ROPE_HALF, 1), pltpu.roll(x, ROPE_HALF, 1))
        outs.append(x * cos + partner * sin)
    return jnp.concatenate(outs, axis=1)


def _proj_kernel(x_ref, g_ref, w_ref, wl_ref, fb_ref, cos_ref, sin_ref,
                 qkv_ref, st_ref, logf_ref, h_ref):
    j = pl.program_id(1)

    @pl.when(j == 0)
    def _():
        hb = _rmsnorm_rows(x_ref[...], g_ref[...]).astype(BF16)
        h_ref[...] = hb
        logf_ref[...] = _log_sigmoid(_dot(hb, wl_ref[...]) + fb_ref[...])

    acc = _dot(h_ref[...], w_ref[...])
    is_rope = jnp.logical_or(j == TILE_CQ, j == TILE_CK)

    @pl.when(is_rope)
    def _():
        y = _rope_tile(acc, cos_ref[...], sin_ref[...])
        st_ref[...] = y
        qkv_ref[...] = (y * jnp.where(j == TILE_CQ, DIFF_HALF ** -0.5, 1.0)).astype(BF16)

    @pl.when(jnp.logical_not(is_rope))
    def _():
        st_ref[...] = acc
        qkv_ref[...] = acc.astype(BF16)


def _proj(x, g, w_qkv, w_logit, fox_b, cos_t, sin_t, layer, tm):
    m, d = x.shape
    tn = GROUP_W
    nper = cos_t.shape[0] // tm
    grid = (m // tm, QKV_W // tn)
    st_map = lambda i, j: (i, 2 * (j // 3) + jnp.maximum(j % 3 - 1, 0))
    return pl.pallas_call(
        _proj_kernel,
        grid=grid,
        in_specs=[
            pl.BlockSpec((tm, d), lambda i, j: (i, 0)),
            pl.BlockSpec((None, 1, d), lambda i, j: (layer, 0, 0)),
            pl.BlockSpec((None, d, tn), lambda i, j: (layer, 0, j)),
            pl.BlockSpec((None, d, LANES), lambda i, j: (layer, 0, 0)),
            pl.BlockSpec((None, 1, LANES), lambda i, j: (layer, 0, 0)),
            pl.BlockSpec((tm, LANES), lambda i, j: (i % nper, 0)),
            pl.BlockSpec((tm, LANES), lambda i, j: (i % nper, 0)),
        ],
        out_specs=[
            pl.BlockSpec((tm, tn), lambda i, j: (i, j)),
            pl.BlockSpec((tm, tn), st_map),
            pl.BlockSpec((tm, LANES), lambda i, j: (i, 0)),
        ],
        out_shape=[
            jax.ShapeDtypeStruct((m, QKV_W), BF16),
            jax.ShapeDtypeStruct((m, 2 * N_GROUPS * GROUP_W), F32),
            jax.ShapeDtypeStruct((m, LANES), F32),
        ],
        scratch_shapes=[pltpu.VMEM((tm, d), BF16)],
        compiler_params=_cparams("parallel", "arbitrary"),
        name="proj",
    )(x, g, w_qkv, w_logit, fox_b, cos_t, sin_t)


def _resid_mm_kernel(*refs, widths):
    n_in = len(widths)
    x_ref, a_refs, w_ref, o_ref = refs[0], refs[1:1 + n_in], refs[1 + n_in], refs[2 + n_in]
    acc = x_ref[...]
    off = 0
    for a_ref, wd in zip(a_refs, widths):
        acc = acc + _dot(a_ref[...], w_ref[off:off + wd, :])
        off += wd
    o_ref[...] = acc


def _resid_mm(x, a_list, w, layer, tm, tn):
    m, n = x.shape
    widths = tuple(a.shape[1] for a in a_list)
    k = sum(widths)
    in_specs = [pl.BlockSpec((tm, tn), lambda i, j: (i, j))]
    in_specs += [pl.BlockSpec((tm, wd), lambda i, j: (i, 0)) for wd in widths]
    in_specs += [pl.BlockSpec((None, k, tn), lambda i, j: (layer, 0, j))]
    return pl.pallas_call(
        functools.partial(_resid_mm_kernel, widths=widths),
        grid=(m // tm, n // tn),
        in_specs=in_specs,
        out_specs=pl.BlockSpec((tm, tn), lambda i, j: (i, j)),
        out_shape=jax.ShapeDtypeStruct((m, n), F32),
        compiler_params=_cparams("parallel", "arbitrary"),
        name="resid_mm",
    )(x, *a_list, w)


def _up_kernel(x_ref, g_ref, w_ref, u_ref, h_ref):
    @pl.when(pl.program_id(1) == 0)
    def _():
        h_ref[...] = _rmsnorm_rows(x_ref[...], g_ref[...]).astype(BF16)

    r = jnp.maximum(_dot(h_ref[...], w_ref[...]), 0.0)
    u_ref[...] = (r * r).astype(BF16)


def _up(x, g, w_up, layer, tm, tn):
    m, d = x.shape
    f = w_up.shape[2]
    return pl.pallas_call(
        _up_kernel,
        grid=(m // tm, f // tn),
        in_specs=[
            pl.BlockSpec((tm, d), lambda i, j: (i, 0)),
            pl.BlockSpec((None, 1, d), lambda i, j: (layer, 0, 0)),
            pl.BlockSpec((None, d, tn), lambda i, j: (layer, 0, j)),
        ],
        out_specs=pl.BlockSpec((tm, tn), lambda i, j: (i, j)),
        out_shape=jax.ShapeDtypeStruct((m, f), BF16),
        scratch_shapes=[pltpu.VMEM((tm, d), BF16)],
        compiler_params=_cparams("parallel", "arbitrary"),
        name="mlp_up",
    )(x, g, w_up)


def _final_norm_kernel(x_ref, g_ref, o_ref):
    o_ref[...] = _rmsnorm_rows(x_ref[...], g_ref[...])


def _final_norm(x, g, tm):
    m, d = x.shape
    return pl.pallas_call(
        _final_norm_kernel,
        grid=(m // tm,),
        in_specs=[pl.BlockSpec((tm, d), lambda i: (i, 0)), pl.BlockSpec((1, d), lambda i: (0, 0))],
        out_specs=pl.BlockSpec((tm, d), lambda i: (i, 0)),
        out_shape=jax.ShapeDtypeStruct((m, d), F32),
        compiler_params=_cparams("parallel"),
        name="final_norm",
    )(x, g)


def _flash_init(rows):
    return (jnp.full((rows, 1), NEG_INF, F32), jnp.zeros((rows, 1), F32), jnp.zeros((rows, HEAD_DIM), F32))


def _flash_update(q, k, v, bias, mask, scale, carry):
    m, l, acc = carry
    s = _dot_nt(q, k)
    if scale != 1.0:
        s = s * scale
    if bias is not None:
        s = s + bias
    if mask is not None:
        s = jnp.where(mask, s, NEG_INF)
    m_new = jnp.maximum(m, jnp.max(s, axis=-1, keepdims=True))
    alpha = jnp.exp(m - m_new)
    p = jnp.exp(s - m_new)
    l = alpha * l + jnp.sum(p, axis=-1, keepdims=True)
    acc = alpha * acc + _dot(p.astype(BF16), v)
    return m_new, l, acc


def _stack_diff_q(q):
    lane = lax.broadcasted_iota(jnp.int32, q.shape, 1)
    zero = jnp.zeros_like(q)
    return jnp.concatenate([jnp.where(lane < DIFF_HALF, q, zero), jnp.where(lane >= DIFF_HALF, q, zero)], axis=0)


def _diff_finish(carry, tq, lam_ref, li_ref, g_ref):
    _, l, acc = carry
    o = acc / l
    lv = lam_ref[...]
    lam = (jnp.exp(jnp.sum(lv[0:1] * lv[1:2], axis=-1, keepdims=True))
           - jnp.exp(jnp.sum(lv[2:3] * lv[3:4], axis=-1, keepdims=True)) + li_ref[0])
    d = o[:tq] - lam * o[tq:]
    return _rmsnorm_rows(d, g_ref[...]) * (1.0 - li_ref[0])


def _sb_block(q, k, v, tri2, mask, scale, run, acc):
    z = _dot_nt(q, k) * scale
    lom = -(jnp.maximum(z, 0.0) + jnp.log1p(jnp.exp(-jnp.abs(z))))
    if mask is not None:
        lom = jnp.where(mask, lom, 0.0)
    hi = lom.astype(BF16)
    lo = (lom - hi.astype(F32)).astype(BF16)
    sums = _dot(jnp.concatenate([hi, lo], axis=1), tri2)
    a = jnp.exp(z + sums[:, :LANES] + run)
    if mask is not None:
        a = jnp.where(mask, a, 0.0)
    return run + sums[:, LANES:], acc + _dot(a.astype(BF16), v)


def _tri2():
    j = np.arange(2 * LANES)[:, None] % LANES
    s = np.arange(2 * LANES)[None, :]
    return jnp.asarray(np.where(s < LANES, j >= s, True), dtype=BF16)


def _fox_prompt_kernel(q_ref, k_ref, v_ref, fk_ref, o_ref, *, t, scale):
    i = pl.program_id(2)
    q = q_ref[...]

    def step(j, carry, mask):
        ks = pl.multiple_of(j * t, t)
        return _flash_update(q, k_ref[pl.ds(ks, t), :], v_ref[pl.ds(ks, t), :], -fk_ref[j], mask, scale, carry)

    carry = lax.fori_loop(0, i, lambda j, c: step(j, c, None), _flash_init(t))
    rows = lax.broadcasted_iota(jnp.int32, (t, t), 0)
    cols = lax.broadcasted_iota(jnp.int32, (t, t), 1)
    _, l, acc = step(i, carry, cols <= rows)
    o_ref[...] = (acc / l).astype(BF16)


def _diff_prompt_kernel(q_ref, k_ref, v_ref, lam_ref, g_ref, li_ref, o_ref, *, t):
    i = pl.program_id(2)
    q2 = _stack_diff_q(q_ref[...])

    def step(j, carry, mask):
        ks = pl.multiple_of(j * t, t)
        return _flash_update(q2, k_ref[pl.ds(ks, t), :], v_ref[pl.ds(ks, t), :], None, mask, 1.0, carry)

    carry = lax.fori_loop(0, i, lambda j, c: step(j, c, None), _flash_init(2 * t))
    rows = lax.broadcasted_iota(jnp.int32, (2 * t, t), 0) % t
    cols = lax.broadcasted_iota(jnp.int32, (2 * t, t), 1)
    carry = step(i, carry, (cols // CHUNK) <= (rows // CHUNK))
    o_ref[...] = _diff_finish(carry, t, lam_ref, li_ref, g_ref).astype(BF16)


def _sb_prompt_kernel(q_ref, k_ref, v_ref, tri_ref, o_ref, *, tq, scale):
    i = pl.program_id(2)
    q = q_ref[...]
    tri2 = tri_ref[...]
    nd = tq // LANES
    run = jnp.zeros((tq, LANES), F32)
    acc = jnp.zeros((tq, HEAD_DIM), F32)
    rows = lax.broadcasted_iota(jnp.int32, (tq, LANES), 0)
    cols = lax.broadcasted_iota(jnp.int32, (tq, LANES), 1)
    for d in reversed(range(nd)):
        ks = pl.multiple_of(i * tq + d * LANES, LANES)
        run, acc = _sb_block(q, k_ref[pl.ds(ks, LANES), :], v_ref[pl.ds(ks, LANES), :], tri2,
                             cols + d * LANES < rows, scale, run, acc)
    nfull = i * nd

    def body(n, c):
        ks = pl.multiple_of((nfull - 1 - n) * LANES, LANES)
        return _sb_block(q, k_ref[pl.ds(ks, LANES), :], v_ref[pl.ds(ks, LANES), :], tri2, None, scale, *c)

    run, acc = lax.fori_loop(0, nfull, body, (run, acc))
    o_ref[...] = acc.astype(BF16)


def _band_prompt_kernel(q_ref, kp_ref, kc_ref, vp_ref, vc_ref, bias_ref, o_ref, *, t, scale):
    i = pl.program_id(2)
    q = q_ref[...]
    sp = _dot_nt(q, kp_ref[...]) * scale + bias_ref[:, :t]
    sp = jnp.where(i > 0, sp, NEG_INF)
    sc = _dot_nt(q, kc_ref[...]) * scale + bias_ref[:, t:]
    m = jnp.maximum(jnp.max(sp, axis=-1, keepdims=True), jnp.max(sc, axis=-1, keepdims=True))
    pp = jnp.exp(sp - m)
    pc = jnp.exp(sc - m)
    l = jnp.sum(pp, axis=-1, keepdims=True) + jnp.sum(pc, axis=-1, keepdims=True)
    o = _dot(pp.astype(BF16), vp_ref[...]) + _dot(pc.astype(BF16), vc_ref[...])
    o_ref[...] = (o / l).astype(BF16)


def _col(group, comp, h):
    return group * 3 * HEADS + comp * HEADS + h


def _prompt_call(kern, qkv, group, b, seq, tq, extra_in, extra_specs, name):
    nq = seq // tq
    m = qkv.shape[0]
    in_specs = [
        pl.BlockSpec((tq, LANES), lambda bi, h, i: (bi * nq + i, _col(group, 0, h))),
        pl.BlockSpec((seq, LANES), lambda bi, h, i: (bi, _col(group, 1, h))),
        pl.BlockSpec((seq, LANES), lambda bi, h, i: (bi, _col(group, 2, h))),
    ] + extra_specs
    return pl.pallas_call(
        kern,
        grid=(b, HEADS, nq),
        in_specs=in_specs,
        out_specs=pl.BlockSpec((tq, LANES), lambda bi, h, i: (bi * nq + i, h)),
        out_shape=jax.ShapeDtypeStruct((m, GROUP_W), BF16),
        compiler_params=_cparams("parallel", "parallel", "arbitrary"),
        name=name,
    )(qkv, qkv, qkv, *extra_in)


def _fox_prompt(qkv, cum_f, b, seq, t):
    nk = seq // t
    fk = jnp.swapaxes(cum_f, 1, 2).reshape(b * HEADS, nk, 1, t)
    spec = pl.BlockSpec((None, nk, 1, t), lambda bi, h, i: (bi * HEADS + h, 0, 0, 0))
    kern = functools.partial(_fox_prompt_kernel, t=t, scale=HEAD_DIM ** -0.5)
    return _prompt_call(kern, qkv, 0, b, seq, t, [fk], [spec], "fox_prompt")


def _diff_prompt(qkv, lam, dng, li, layer, b, seq, t):
    specs = [
        pl.BlockSpec((None, 4, DIFF_HALF), lambda bi, h, i: (layer, 0, 0)),
        pl.BlockSpec((None, 1, HEAD_DIM), lambda bi, h, i: (layer, 0, 0)),
        pl.BlockSpec(memory_space=pltpu.SMEM),
    ]
    kern = functools.partial(_diff_prompt_kernel, t=t)
    return _prompt_call(kern, qkv, 2, b, seq, t, [lam, dng, li], specs, "diff_prompt")


def _sb_prompt(qkv, tri2, b, seq, tq):
    spec = pl.BlockSpec((2 * LANES, 2 * LANES), lambda bi, h, i: (0, 0))
    kern = functools.partial(_sb_prompt_kernel, tq=tq, scale=HEAD_DIM ** -0.5)
    return _prompt_call(kern, qkv, 3, b, seq, tq, [tri2], [spec], "sb_prompt")


def _band_prompt(qkv, table, layer, b, seq, t):
    nq = seq // t
    m = qkv.shape[0]
    cur = lambda comp: (lambda bi, h, i: (bi * nq + i, _col(1, comp, h)))
    prev = lambda comp: (lambda bi, h, i: (bi * nq + jnp.maximum(i - 1, 0), _col(1, comp, h)))
    return pl.pallas_call(
        functools.partial(_band_prompt_kernel, t=t, scale=HEAD_DIM ** -0.5),
        grid=(b, HEADS, nq),
        in_specs=[
            pl.BlockSpec((t, LANES), cur(0)),
            pl.BlockSpec((t, LANES), prev(1)),
            pl.BlockSpec((t, LANES), cur(1)),
            pl.BlockSpec((t, LANES), prev(2)),
            pl.BlockSpec((t, LANES), cur(2)),
            pl.BlockSpec((None, None, t, 2 * t), lambda bi, h, i: (layer, h, 0, 0)),
        ],
        out_specs=pl.BlockSpec((t, LANES), lambda bi, h, i: (bi * nq + i, h)),
        out_shape=jax.ShapeDtypeStruct((m, GROUP_W), BF16),
        compiler_params=_cparams("parallel", "parallel", "arbitrary"),
        name="band_prompt",
    )(qkv, qkv, qkv, qkv, qkv, table)


def _fox_sample_kernel(q_ref, kn_ref, vn_ref, kc_ref, vc_ref, fkc_ref, fkn_ref, o_ref, *, ts, scale):
    q = q_ref[...]
    carry = _flash_update(q, kc_ref[...].astype(BF16), vc_ref[...].astype(BF16), -fkc_ref[...], None, scale,
                          _flash_init(ts))
    rows = lax.broadcasted_iota(jnp.int32, (ts, ts), 0)
    cols = lax.broadcasted_iota(jnp.int32, (ts, ts), 1)
    _, l, acc = _flash_update(q, kn_ref[...], vn_ref[...], -fkn_ref[...], cols <= rows, scale, carry)
    o_ref[...] = (acc / l).astype(BF16)


def _band_sample_kernel(q_ref, kn_ref, vn_ref, kc_ref, vc_ref, bias_ref, o_ref, *, ts, lb, scale):
    q = q_ref[...]
    carry = _flash_update(q, kc_ref[...].astype(BF16), vc_ref[...].astype(BF16), bias_ref[:, :lb], None, scale,
                          _flash_init(ts))
    _, l, acc = _flash_update(q, kn_ref[...], vn_ref[...], bias_ref[:, lb:], None, scale, carry)
    o_ref[...] = (acc / l).astype(BF16)


def _diff_sample_kernel(q_ref, kn_ref, vn_ref, kc_ref, vc_ref, lam_ref, g_ref, li_ref, o_ref, *, ts):
    q2 = _stack_diff_q(q_ref[...])
    carry = _flash_update(q2, kc_ref[...].astype(BF16), vc_ref[...].astype(BF16), None, None, 1.0,
                          _flash_init(2 * ts))
    carry = _flash_update(q2, kn_ref[...], vn_ref[...], None, None, 1.0, carry)
    o_ref[...] = _diff_finish(carry, ts, lam_ref, li_ref, g_ref).astype(BF16)


def _sb_sample_kernel(q_ref, kn_ref, vn_ref, kc_ref, vc_ref, tri_ref, o_ref, *, ts, past, scale):
    q = q_ref[...]
    tri2 = tri_ref[...]
    run = jnp.zeros((ts, LANES), F32)
    acc = jnp.zeros((ts, HEAD_DIM), F32)
    pad = jnp.zeros((LANES - ts, HEAD_DIM), BF16)
    rows = lax.broadcasted_iota(jnp.int32, (ts, LANES), 0)
    cols = lax.broadcasted_iota(jnp.int32, (ts, LANES), 1)
    run, acc = _sb_block(q, jnp.concatenate([kn_ref[...], pad], axis=0), jnp.concatenate([vn_ref[...], pad], axis=0),
                         tri2, cols < rows, scale, run, acc)
    for n in reversed(range(past // LANES)):
        kc = kc_ref[n * LANES:(n + 1) * LANES, :].astype(BF16)
        vc = vc_ref[n * LANES:(n + 1) * LANES, :].astype(BF16)
        run, acc = _sb_block(q, kc, vc, tri2, None, scale, run, acc)
    o_ref[...] = acc.astype(BF16)


def _sample_call(kern, qkv, group, cache_k, cache_v, layer, bs, ts, extra_in, extra_specs, name):
    m = qkv.shape[0]
    rows = cache_k.shape[1]
    new = lambda comp: pl.BlockSpec((ts, LANES), lambda b, h: (b, _col(group, comp, h)))
    cache = pl.BlockSpec((None, rows, LANES), lambda b, h: (layer * bs + b, 0, h))
    return pl.pallas_call(
        kern,
        grid=(bs, HEADS),
        in_specs=[new(0), new(1), new(2), cache, cache] + extra_specs,
        out_specs=pl.BlockSpec((ts, LANES), lambda b, h: (b, h)),
        out_shape=jax.ShapeDtypeStruct((m, GROUP_W), BF16),
        compiler_params=_cparams("parallel", "arbitrary"),
        name=name,
    )(qkv, qkv, qkv, cache_k, cache_v, *extra_in)


def _rope_tables(pos):
    inv = ROPE_THETA ** (-jnp.arange(ROPE_HALF, dtype=F32) / ROPE_HALF)
    ang = pos.astype(F32)[:, None] * inv[None, :]
    cos, sin = jnp.cos(ang), jnp.sin(ang)
    return jnp.tile(cos, (1, 4)), jnp.concatenate([-sin, sin, -sin, sin], axis=1)


def _band_table(rel, nrows, ncols, q_off):
    qp = q_off + np.arange(nrows)[:, None]
    kp = np.arange(ncols)[None, :]
    idx = np.clip(qp - kp, -REL_CLIP, REL_CLIP) + REL_CLIP
    qc, kc = qp // CHUNK, kp // CHUNK
    vis = (kc <= qc) & (kc >= qc - BAND_CHUNKS)
    return jnp.where(jnp.asarray(vis), rel.astype(F32)[:, :, jnp.asarray(idx)], NEG_INF)


def _row_tile(m, want):
    t = min(m, want)
    assert m % t == 0
    return t


def kernel(x_prompt, x_sample, cache_a_k, cache_a_v, cache_a_logf, cache_b_k, cache_b_v, cache_c_k, cache_c_v, cache_d_k, cache_d_v, attn_norm_g, w_in, fox_b, band_rel_bias, diff_lambda, diff_norm_g, w_out, mlp_norm_g, w_up, w_down, final_norm_g):
    b, seq, d = x_prompt.shape
    bs, ts, _ = x_sample.shape
    depth = w_in.shape[0]
    past = cache_a_k.shape[2]
    lb = cache_b_k.shape[2]
    lbp = min(BAND_PAST, seq)
    mp, ms = b * seq, bs * ts
    assert w_in.shape[2] == QKV_W + HEADS and d == N_GROUPS * GROUP_W
    assert ts == CHUNK and past % LANES == 0 and past % CHUNK == 0 and lb == BAND_PAST

    tm_p = _row_tile(seq, 512)
    tm_s = _row_tile(ms, 512)
    assert tm_s % ts == 0
    t_fox = _row_tile(seq, 512)
    t_band = _row_tile(seq, 512)
    t_diff = _row_tile(seq, 256)
    t_sb = _row_tile(seq, 256)
    assert t_band >= BAND_PAST

    w_qkv = w_in[:, :, :QKV_W].astype(BF16)
    w_logit = jnp.pad(w_in[:, :, QKV_W:], ((0, 0), (0, 0), (0, LANES - HEADS))).astype(BF16)
    fox_b_pad = jnp.pad(fox_b.astype(F32), ((0, 0), (0, LANES - HEADS)))[:, None, :]
    w_out_b, w_up_b, w_down_b = w_out.astype(BF16), w_up.astype(BF16), w_down.astype(BF16)
    attn_g = attn_norm_g.astype(F32)[:, None, :]
    mlp_g = mlp_norm_g.astype(F32)[:, None, :]
    dng = diff_norm_g.astype(F32)[:, None, :]
    lam_rows = diff_lambda.astype(F32)
    cos_p, sin_p = _rope_tables(jnp.arange(seq))
    cos_s, sin_s = _rope_tables(past + jnp.arange(ts))
    cos_s, sin_s = jnp.tile(cos_s, (tm_s // ts, 1)), jnp.tile(sin_s, (tm_s // ts, 1))
    table_p = _band_table(band_rel_bias, t_band, 2 * t_band, t_band)
    table_s = _band_table(band_rel_bias, ts, lb + ts, lb)
    tri2 = _tri2()

    flat = lambda c: c.reshape(depth * bs, c.shape[2], GROUP_W)
    ca_k, ca_v, cb_k, cb_v = flat(cache_a_k), flat(cache_a_v), flat(cache_b_k), flat(cache_b_v)
    cc_k, cc_v, cd_k, cd_v = flat(cache_c_k), flat(cache_c_v), flat(cache_d_k), flat(cache_d_v)

    xp = x_prompt.reshape(mp, d)
    xs = x_sample.reshape(ms, d)
    scale = HEAD_DIM ** -0.5
    p_states, s_states = [], []
    for l in range(depth):
        lam_init = 0.8 - 0.6 * float(np.exp(-0.3 * l))
        li = jnp.asarray([lam_init], F32)

        qkv, st, logf = _proj(xp, attn_g, w_qkv, w_logit, fox_b_pad, cos_p, sin_p, l, tm_p)
        logf = logf[:, :HEADS].reshape(b, seq, HEADS)
        cum_f = jnp.cumsum(logf, axis=1)
        oa = _fox_prompt(qkv, cum_f, b, seq, t_fox)
        ob = _band_prompt(qkv, table_p, l, b, seq, t_band)
        oc = _diff_prompt(qkv, lam_rows, dng, li, l, b, seq, t_diff)
        od = _sb_prompt(qkv, tri2, b, seq, t_sb)
        xp = _resid_mm(xp, [oa, ob, oc, od], w_out_b, l, tm_p, 1024)
        u = _up(xp, mlp_g, w_up_b, l, tm_p, 1024)
        xp = _resid_mm(xp, [u], w_down_b, l, tm_p, 512)
        sl = lambda n: st[:, n * GROUP_W:(n + 1) * GROUP_W].reshape(b, seq, HEADS, HEAD_DIM)
        p_states.append((sl(0), sl(1), logf, sl(2)[:, seq - lbp:], sl(3)[:, seq - lbp:],
                         sl(4), sl(5), sl(6), sl(7)))

        qkv, st, logf = _proj(xs, attn_g, w_qkv, w_logit, fox_b_pad, cos_s, sin_s, l, tm_s)
        logf = logf[:, :HEADS].reshape(bs, ts, HEADS)
        cum_f = jnp.cumsum(jnp.concatenate([cache_a_logf[l].astype(F32), logf], axis=1), axis=1)
        fk = jnp.swapaxes(cum_f, 1, 2).reshape(bs * HEADS, 1, past + ts)
        fk_specs = [pl.BlockSpec((None, 1, past), lambda bi, h: (bi * HEADS + h, 0, 0)),
                    pl.BlockSpec((None, 1, ts), lambda bi, h: (bi * HEADS + h, 0, 0))]
        oa = _sample_call(functools.partial(_fox_sample_kernel, ts=ts, scale=scale), qkv, 0, ca_k, ca_v, l, bs, ts,
                          [fk[:, :, :past], fk[:, :, past:]], fk_specs, "fox_sample")
        ob = _sample_call(functools.partial(_band_sample_kernel, ts=ts, lb=lb, scale=scale), qkv, 1, cb_k, cb_v,
                          l, bs, ts, [table_s],
                          [pl.BlockSpec((None, None, ts, lb + ts), lambda bi, h: (l, h, 0, 0))], "band_sample")
        oc = _sample_call(functools.partial(_diff_sample_kernel, ts=ts), qkv, 2, cc_k, cc_v, l, bs, ts,
                          [lam_rows, dng, li],
                          [pl.BlockSpec((None, 4, DIFF_HALF), lambda bi, h: (l, 0, 0)),
                           pl.BlockSpec((None, 1, HEAD_DIM), lambda bi, h: (l, 0, 0)),
                           pl.BlockSpec(memory_space=pltpu.SMEM)], "diff_sample")
        od = _sample_call(functools.partial(_sb_sample_kernel, ts=ts, past=past, scale=scale), qkv, 3, cd_k, cd_v,
                          l, bs, ts, [tri2], [pl.BlockSpec((2 * LANES, 2 * LANES), lambda bi, h: (0, 0))],
                          "sb_sample")
        xs = _resid_mm(xs, [oa, ob, oc, od], w_out_b, l, tm_s, 1024)
        u = _up(xs, mlp_g, w_up_b, l, tm_s, 1024)
        xs = _resid_mm(xs, [u], w_down_b, l, tm_s, 512)
        sl = lambda n: st[:, n * GROUP_W:(n + 1) * GROUP_W].reshape(bs, ts, HEADS, HEAD_DIM)
        s_states.append((sl(0), sl(1), logf,
                         jnp.concatenate([cache_b_k[l][:, ts:], sl(2)], axis=1),
                         jnp.concatenate([cache_b_v[l][:, ts:], sl(3)], axis=1),
                         sl(4), sl(5), sl(6), sl(7)))

    fin_g = final_norm_g.astype(F32)[None, :]
    y_prompt = _final_norm(xp, fin_g, tm_p).reshape(b, seq, d)
    y_sample = _final_norm(xs, fin_g, tm_s).reshape(bs, ts, d)
    p_out = [jnp.stack(s) for s in zip(*p_states)]
    s_out = [jnp.stack(s) for s in zip(*s_states)]
    return (y_prompt, y_sample, *p_out, *s_out)
```

```python
import functools
import math

import numpy as np
import jax
import jax.numpy as jnp
from jax import lax
from jax.experimental import pallas as pl
from jax.experimental.pallas import tpu as pltpu

F32 = jnp.float32
BF16 = jnp.bfloat16

LANES = 128
HEAD_DIM = 128
HEADS = 4
GROUP_W = HEADS * HEAD_DIM
N_GROUPS = 4
QKV_W = 3 * N_GROUPS * GROUP_W
CHUNK = 64
BAND_CHUNKS = 8
BAND_PAST = BAND_CHUNKS * CHUNK
REL_CLIP = 128
DIFF_HALF = HEAD_DIM // 2
ROPE_HALF = DIFF_HALF // 2
ROPE_THETA = 10000.0
EPS = 1e-6
NEG_INF = -1e30
LOG2E = math.log2(math.e)
SCALE = HEAD_DIM ** -0.5
VMEM_LIMIT = 48 * 1024 * 1024

EXP2_ZERO = -160.0

TILE_CQ = 2 * 3 + 0
TILE_CK = 2 * 3 + 1
Q_SCALE = SCALE * LOG2E
Q_SCALE_DIFF = DIFF_HALF ** -0.5 * LOG2E


def _cparams(*sem):
    return pltpu.CompilerParams(dimension_semantics=sem, vmem_limit_bytes=VMEM_LIMIT)


def _dot(a, b):
    return jnp.dot(a, b, preferred_element_type=F32)


def _dot_nt(a, b):
    return lax.dot_general(a, b, (((1,), (1,)), ((), ())), preferred_element_type=F32)


def _rmsnorm_rows(x, g):
    return x * lax.rsqrt(jnp.mean(x * x, axis=-1, keepdims=True) + EPS) * g


def _log_sigmoid(z):
    return jnp.minimum(z, 0.0) - jnp.log1p(jnp.exp(-jnp.abs(z)))


def _row_norm(x):
    xf = x.astype(F32)
    return jnp.sqrt(jnp.sum(xf * xf, axis=-1, keepdims=True))


def _rope_tile(y, cos, sin):
    lane = lax.broadcasted_iota(jnp.int32, (1, LANES), 1)
    first = (lane % DIFF_HALF) < ROPE_HALF
    outs = []
    for h in range(HEADS):
        x = y[:, h * LANES:(h + 1) * LANES]
        partner = jnp.where(first, pltpu.roll(x, LANES - ROPE_HALF, 1), pltpu.roll(x, ROPE_HALF, 1))
        outs.append(x * cos + partner * sin)
    return jnp.concatenate(outs, axis=1)


def _proj_kernel(*refs, n_alias, nper, tail_rows):
    x_ref, g_ref, w_ref, wl_ref, fb_ref, cos_ref, sin_ref = refs[:7]
    qkv_ref, logf_ref = refs[7 + n_alias:9 + n_alias]
    st_refs = refs[9 + n_alias:17 + n_alias]
    h_ref = refs[17 + n_alias]
    i, j = pl.program_id(0), pl.program_id(1)
    tm = x_ref.shape[0]

    @pl.when(j == 0)
    def _():
        hb = _rmsnorm_rows(x_ref[...], g_ref[...]).astype(BF16)
        h_ref[...] = hb
        logf_ref[...] = _log_sigmoid(_dot(hb, wl_ref[...]) + fb_ref[...])

    acc = _dot(h_ref[...], w_ref[...])
    is_q = j % 3 == 0

    @pl.when(j == TILE_CQ)
    def _():
        qkv_ref[...] = (_rope_tile(acc, cos_ref[...], sin_ref[...]) * Q_SCALE_DIFF).astype(BF16)

    @pl.when(j == TILE_CK)
    def _():
        y = _rope_tile(acc, cos_ref[...], sin_ref[...])
        qkv_ref[...] = y.astype(BF16)
        st_refs[4][...] = y

    @pl.when(jnp.logical_and(is_q, j != TILE_CQ))
    def _():
        qkv_ref[...] = (acc * Q_SCALE).astype(BF16)

    @pl.when(jnp.logical_and(jnp.logical_not(is_q), j != TILE_CK))
    def _():
        qkv_ref[...] = acc.astype(BF16)

    for n, st_ref in enumerate(st_refs):
        if n == 4:
            continue
        cond = j == 3 * (n // 2) + 1 + n % 2
        if tail_rows and n in (2, 3):
            @pl.when(jnp.logical_and(cond, i % nper == nper - 1))
            def _(st_ref=st_ref):
                st_ref[...] = acc[tm - tail_rows:, :]
        else:
            @pl.when(cond)
            def _(st_ref=st_ref):
                st_ref[...] = acc


def _proj(x, g, w_qkv, w_logit, fox_b, cos_t, sin_t, layer, depth, tm, tail_rows, bufs):
    m, d = x.shape
    tn = GROUP_W
    nt = m // tm
    nper = cos_t.shape[0] // tm
    full = pl.BlockSpec((tm, tn), lambda i, j: (layer * nt + i, 0))
    full_shape = jax.ShapeDtypeStruct((depth * m, tn), F32)
    if tail_rows:
        nstream = nt // nper
        bspec = pl.BlockSpec((tail_rows, tn), lambda i, j: (layer * nstream + i // nper, 0))
        bshape = jax.ShapeDtypeStruct((depth * nstream * tail_rows, tn), F32)
    else:
        bspec = pl.BlockSpec((tm, tn), lambda i, j: (i, 0))
        bshape = jax.ShapeDtypeStruct((m, tn), F32)
    st_specs = [full, full, bspec, bspec, full, full, full, full]
    st_shapes = [full_shape, full_shape, bshape, bshape, full_shape, full_shape, full_shape, full_shape]
    stacked = [n for n in range(8) if tail_rows or n not in (2, 3)]
    alias_in = [bufs[n] for n in stacked] if bufs is not None else []
    aliases = {7 + k: 2 + n for k, n in enumerate(stacked)} if bufs is not None else {}
    outs = pl.pallas_call(
        functools.partial(_proj_kernel, n_alias=len(alias_in), nper=nper, tail_rows=tail_rows),
        grid=(nt, QKV_W // tn),
        in_specs=[
            pl.BlockSpec((tm, d), lambda i, j: (i, 0)),
            pl.BlockSpec((None, 1, d), lambda i, j: (layer, 0, 0)),
            pl.BlockSpec((None, d, tn), lambda i, j: (layer, 0, j)),
            pl.BlockSpec((None, d, LANES), lambda i, j: (layer, 0, 0)),
            pl.BlockSpec((None, 1, LANES), lambda i, j: (layer, 0, 0)),
            pl.BlockSpec((tm, LANES), lambda i, j: (i % nper, 0)),
            pl.BlockSpec((tm, LANES), lambda i, j: (i % nper, 0)),
        ] + [pl.BlockSpec(memory_space=pl.ANY)] * len(alias_in),
        out_specs=[
            pl.BlockSpec((tm, tn), lambda i, j: (i, j)),
            pl.BlockSpec((tm, LANES), lambda i, j: (i, 0)),
        ] + st_specs,
        out_shape=[
            jax.ShapeDtypeStruct((m, QKV_W), BF16),
            jax.ShapeDtypeStruct((m, LANES), F32),
        ] + st_shapes,
        scratch_shapes=[pltpu.VMEM((tm, d), BF16)],
        input_output_aliases=aliases,
        compiler_params=_cparams("parallel", "arbitrary"),
        name="proj",
    )(x, g, w_qkv, w_logit, fox_b, cos_t, sin_t, *alias_in)
    return outs[0], outs[1], list(outs[2:])


def _resid_mm_kernel(*refs, widths):
    n_in = len(widths)
    x_ref, a_refs, w_ref, o_ref = refs[0], refs[1:1 + n_in], refs[1 + n_in], refs[2 + n_in]
    acc = x_ref[...]
    off = 0
    for a_ref, wd in zip(a_refs, widths):
        acc = acc + _dot(a_ref[...], w_ref[off:off + wd, :])
        off += wd
    o_ref[...] = acc


def _resid_mm(x, a_list, w, layer, tm, tn):
    m, n = x.shape
    widths = tuple(a.shape[1] for a in a_list)
    k = sum(widths)
    in_specs = [pl.BlockSpec((tm, tn), lambda i, j: (i, j))]
    in_specs += [pl.BlockSpec((tm, wd), lambda i, j: (i, 0)) for wd in widths]
    in_specs += [pl.BlockSpec((None, k, tn), lambda i, j: (layer, 0, j))]
    return pl.pallas_call(
        functools.partial(_resid_mm_kernel, widths=widths),
        grid=(m // tm, n // tn),
        in_specs=in_specs,
        out_specs=pl.BlockSpec((tm, tn), lambda i, j: (i, j)),
        out_shape=jax.ShapeDtypeStruct((m, n), F32),
        compiler_params=_cparams("parallel", "arbitrary"),
        name="resid_mm",
    )(x, *a_list, w)


def _up_kernel(x_ref, g_ref, w_ref, u_ref, h_ref):
    @pl.when(pl.program_id(1) == 0)
    def _():
        h_ref[...] = _rmsnorm_rows(x_ref[...], g_ref[...]).astype(BF16)

    r = jnp.maximum(_dot(h_ref[...], w_ref[...]), 0.0)
    u_ref[...] = (r * r).astype(BF16)


def _up(x, g, w_up, layer, tm, tn):
    m, d = x.shape
    f = w_up.shape[2]
    return pl.pallas_call(
        _up_kernel,
        grid=(m // tm, f // tn),
        in_specs=[
            pl.BlockSpec((tm, d), lambda i, j: (i, 0)),
            pl.BlockSpec((None, 1, d), lambda i, j: (layer, 0, 0)),
            pl.BlockSpec((None, d, tn), lambda i, j: (layer, 0, j)),
        ],
        out_specs=pl.BlockSpec((tm, tn), lambda i, j: (i, j)),
        out_shape=jax.ShapeDtypeStruct((m, f), BF16),
        scratch_shapes=[pltpu.VMEM((tm, d), BF16)],
        compiler_params=_cparams("parallel", "arbitrary"),
        name="mlp_up",
    )(x, g, w_up)


def _final_norm_kernel(x_ref, g_ref, o_ref):
    o_ref[...] = _rmsnorm_rows(x_ref[...], g_ref[...])


def _final_norm(x, g, tm):
    m, d = x.shape
    return pl.pallas_call(
        _final_norm_kernel,
        grid=(m // tm,),
        in_specs=[pl.BlockSpec((tm, d), lambda i: (i, 0)), pl.BlockSpec((1, d), lambda i: (0, 0))],
        out_specs=pl.BlockSpec((tm, d), lambda i: (i, 0)),
        out_shape=jax.ShapeDtypeStruct((m, d), F32),
        compiler_params=_cparams("parallel"),
        name="final_norm",
    )(x, g)


def _flash_init(rows):
    return (jnp.full((rows, 1), NEG_INF, F32), jnp.zeros((rows, 1), F32), jnp.zeros((rows, HEAD_DIM), F32))


def _flash_update(q, k, v, bias, mask, carry):
    m, l, acc = carry
    s = _dot_nt(q, k)
    if bias is not None:
        s = s + bias
    if mask is not None:
        s = jnp.where(mask, s, NEG_INF)
    m_new = jnp.maximum(m, jnp.max(s, axis=-1, keepdims=True))
    alpha = jnp.exp2(m - m_new)
    p = jnp.exp2(s - m_new)
    l = alpha * l + jnp.sum(p, axis=-1, keepdims=True)
    acc = alpha * acc + _dot(p.astype(BF16), v)
    return m_new, l, acc


def _stack_diff_q(q):
    lane = lax.broadcasted_iota(jnp.int32, q.shape, 1)
    zero = jnp.zeros_like(q)
    return jnp.concatenate([jnp.where(lane < DIFF_HALF, q, zero), jnp.where(lane >= DIFF_HALF, q, zero)], axis=0)


def _diff_finish(carry, tq, lam_ref, li_ref, g_ref):
    _, l, acc = carry
    o = acc / l
    lv = lam_ref[...]
    lam = (jnp.exp(jnp.sum(lv[0:1] * lv[1:2], axis=-1, keepdims=True))
           - jnp.exp(jnp.sum(lv[2:3] * lv[3:4], axis=-1, keepdims=True)) + li_ref[0])
    d = o[:tq] - lam * o[tq:]
    return _rmsnorm_rows(d, g_ref[...]) * (1.0 - li_ref[0])


def _sb_span(q, k, v, tri2, mask, run, acc):
    nsub = k.shape[0] // LANES
    z = _dot_nt(q, k)
    neg_abs = pltpu.bitcast(pltpu.bitcast(z, jnp.uint32) | jnp.uint32(0x80000000), F32)
    sp = jnp.maximum(z, 0.0) + jnp.log2(1.0 + jnp.exp2(neg_abs))
    if mask is not None:
        sp = jnp.where(mask, sp, 0.0)
    hi = sp.astype(BF16)
    lo = (sp - hi.astype(F32)).astype(BF16)
    parts = [None] * nsub
    for n in reversed(range(nsub)):
        c = slice(n * LANES, (n + 1) * LANES)
        sums = _dot(jnp.concatenate([hi[:, c], lo[:, c]], axis=1), tri2)
        parts[n] = z[:, c] + sums[:, :LANES] + run
        run = run + sums[:, LANES:]
    a = jnp.exp2(jnp.concatenate(parts, axis=1))
    if mask is not None:
        a = jnp.where(mask, a, 0.0)
    return run, acc + _dot(a.astype(BF16), v)


def _tri2():
    j = np.arange(2 * LANES)[:, None] % LANES
    s = np.arange(2 * LANES)[None, :]
    return jnp.asarray(np.where(np.where(s < LANES, j >= s, True), -1.0, 0.0), dtype=BF16)


def _key_norm_max(k_ref, kmax_ref):
    @pl.when(pl.program_id(2) == 0)
    def _():
        kmax_ref[...] = jnp.max(_row_norm(k_ref[...]), axis=0, keepdims=True)


def _fox_prompt_kernel(q_ref, k_ref, v_ref, b_ref, o_ref, kmax_ref, *, t):
    i = pl.program_id(2)
    _key_norm_max(k_ref, kmax_ref)
    q = q_ref[...]
    bound = _row_norm(q) * kmax_ref[...]

    def update(j, carry, mask):
        ks = pl.multiple_of(j * t, t)
        return _flash_update(q, k_ref[pl.ds(ks, t), :], v_ref[pl.ds(ks, t), :], b_ref[j], mask, carry)

    def visible(j, m):
        top = jnp.max(b_ref[jnp.maximum(j, 0)], axis=-1, keepdims=True)
        return jnp.max(bound + top - m) > EXP2_ZERO

    rows = lax.broadcasted_iota(jnp.int32, (t, t), 0)
    cols = lax.broadcasted_iota(jnp.int32, (t, t), 1)
    carry = update(i, _flash_init(t), cols <= rows)

    def cond(c):
        return jnp.logical_and(c[0] >= 0, c[1])

    def body(c):
        j = c[0]
        carry = update(j, c[2:], None)
        return (j - 1, visible(j - 1, carry[0])) + carry

    _, _, _, l, acc = lax.while_loop(cond, body, (i - 1, visible(i - 1, carry[0])) + carry)
    o_ref[...] = (acc / l).astype(BF16)


def _diff_prompt_kernel(q_ref, k_ref, v_ref, lam_ref, g_ref, li_ref, o_ref, *, t):
    i = pl.program_id(2)
    q2 = _stack_diff_q(q_ref[...])

    def step(j, carry, mask):
        ks = pl.multiple_of(j * t, t)
        return _flash_update(q2, k_ref[pl.ds(ks, t), :], v_ref[pl.ds(ks, t), :], None, mask, carry)

    carry = lax.fori_loop(0, i, lambda j, c: step(j, c, None), _flash_init(2 * t))
    rows = lax.broadcasted_iota(jnp.int32, (2 * t, t), 0) % t
    cols = lax.broadcasted_iota(jnp.int32, (2 * t, t), 1)
    carry = step(i, carry, (cols // CHUNK) <= (rows // CHUNK))
    o_ref[...] = _diff_finish(carry, t, lam_ref, li_ref, g_ref).astype(BF16)


def _sb_prompt_kernel(q_ref, k_ref, v_ref, tri_ref, o_ref, kmax_ref, *, t):
    i = pl.program_id(2)
    _key_norm_max(k_ref, kmax_ref)
    q = q_ref[...]
    tri2 = tri_ref[...]
    bound = _row_norm(q) * kmax_ref[...]
    rows = lax.broadcasted_iota(jnp.int32, (t, t), 0)
    cols = lax.broadcasted_iota(jnp.int32, (t, t), 1)

    def span(j, mask, run, acc):
        ks = pl.multiple_of(j * t, t)
        return _sb_span(q, k_ref[pl.ds(ks, t), :], v_ref[pl.ds(ks, t), :], tri2, mask, run, acc)

    def visible(run):
        return jnp.max(bound + run[:, :1]) > EXP2_ZERO

    run, acc = span(i, cols < rows, jnp.zeros((t, LANES), F32), jnp.zeros((t, HEAD_DIM), F32))

    def cond(c):
        return jnp.logical_and(c[0] >= 0, c[1])

    def body(c):
        run, acc = span(c[0], None, c[2], c[3])
        return c[0] - 1, visible(run), run, acc

    _, _, _, acc = lax.while_loop(cond, body, (i - 1, visible(run), run, acc))
    o_ref[...] = acc.astype(BF16)


def _band_prompt_kernel(q_ref, kp_ref, kc_ref, vp_ref, vc_ref, bias_ref, o_ref, *, t):
    i = pl.program_id(2)
    q = q_ref[...]
    sp = _dot_nt(q, kp_ref[...]) + bias_ref[:, :t]
    sp = jnp.where(i > 0, sp, NEG_INF)
    sc = _dot_nt(q, kc_ref[...]) + bias_ref[:, t:]
    m = jnp.maximum(jnp.max(sp, axis=-1, keepdims=True), jnp.max(sc, axis=-1, keepdims=True))
    pp = jnp.exp2(sp - m)
    pc = jnp.exp2(sc - m)
    l = jnp.sum(pp, axis=-1, keepdims=True) + jnp.sum(pc, axis=-1, keepdims=True)
    o = _dot(pp.astype(BF16), vp_ref[...]) + _dot(pc.astype(BF16), vc_ref[...])
    o_ref[...] = (o / l).astype(BF16)


def _col(group, comp, h):
    return group * 3 * HEADS + comp * HEADS + h


def _prompt_call(kern, qkv, group, b, seq, tq, extra_in, extra_specs, scratch, name):
    nq = seq // tq
    m = qkv.shape[0]
    in_specs = [
        pl.BlockSpec((tq, LANES), lambda bi, h, i: (bi * nq + i, _col(group, 0, h))),
        pl.BlockSpec((seq, LANES), lambda bi, h, i: (bi, _col(group, 1, h))),
        pl.BlockSpec((seq, LANES), lambda bi, h, i: (bi, _col(group, 2, h))),
    ] + extra_specs
    return pl.pallas_call(
        kern,
        grid=(b, HEADS, nq),
        in_specs=in_specs,
        out_specs=pl.BlockSpec((tq, LANES), lambda bi, h, i: (bi * nq + i, h)),
        out_shape=jax.ShapeDtypeStruct((m, GROUP_W), BF16),
        scratch_shapes=scratch,
        compiler_params=_cparams("parallel", "parallel", "arbitrary"),
        name=name,
    )(qkv, qkv, qkv, *extra_in)


def _fox_prompt(qkv, key_bias, b, seq, t):
    nk = seq // t
    kb = jnp.swapaxes(key_bias, 1, 2).reshape(b * HEADS, nk, 1, t)
    spec = pl.BlockSpec((None, nk, 1, t), lambda bi, h, i: (bi * HEADS + h, 0, 0, 0))
    kern = functools.partial(_fox_prompt_kernel, t=t)
    return _prompt_call(kern, qkv, 0, b, seq, t, [kb], [spec], [pltpu.VMEM((1, 1), F32)], "fox_prompt")


def _diff_prompt(qkv, lam, dng, li, layer, b, seq, t):
    specs = [
        pl.BlockSpec((None, 4, DIFF_HALF), lambda bi, h, i: (layer, 0, 0)),
        pl.BlockSpec((None, 1, HEAD_DIM), lambda bi, h, i: (layer, 0, 0)),
        pl.BlockSpec(memory_space=pltpu.SMEM),
    ]
    kern = functools.partial(_diff_prompt_kernel, t=t)
    return _prompt_call(kern, qkv, 2, b, seq, t, [lam, dng, li], specs, [], "diff_prompt")


def _sb_prompt(qkv, tri2, b, seq, t):
    spec = pl.BlockSpec((2 * LANES, 2 * LANES), lambda bi, h, i: (0, 0))
    kern = functools.partial(_sb_prompt_kernel, t=t)
    return _prompt_call(kern, qkv, 3, b, seq, t, [tri2], [spec], [pltpu.VMEM((1, 1), F32)], "sb_prompt")


def _band_prompt(qkv, table, layer, b, seq, t):
    nq = seq // t
    m = qkv.shape[0]
    cur = lambda comp: (lambda bi, h, i: (bi * nq + i, _col(1, comp, h)))
    prev = lambda comp: (lambda bi, h, i: (bi * nq + jnp.maximum(i - 1, 0), _col(1, comp, h)))
    return pl.pallas_call(
        functools.partial(_band_prompt_kernel, t=t),
        grid=(b, HEADS, nq),
        in_specs=[
            pl.BlockSpec((t, LANES), cur(0)),
            pl.BlockSpec((t, LANES), prev(1)),
            pl.BlockSpec((t, LANES), cur(1)),
            pl.BlockSpec((t, LANES), prev(2)),
            pl.BlockSpec((t, LANES), cur(2)),
            pl.BlockSpec((None, None, t, 2 * t), lambda bi, h, i: (layer, h, 0, 0)),
        ],
        out_specs=pl.BlockSpec((t, LANES), lambda bi, h, i: (bi * nq + i, h)),
        out_shape=jax.ShapeDtypeStruct((m, GROUP_W), BF16),
        compiler_params=_cparams("parallel", "parallel", "arbitrary"),
        name="band_prompt",
    )(qkv, qkv, qkv, qkv, qkv, table)


def _fox_sample_kernel(q_ref, kn_ref, vn_ref, kc_ref, vc_ref, bc_ref, bn_ref, o_ref, *, ts):
    q = q_ref[...]
    carry = _flash_update(q, kc_ref[...].astype(BF16), vc_ref[...].astype(BF16), bc_ref[...], None, _flash_init(ts))
    rows = lax.broadcasted_iota(jnp.int32, (ts, ts), 0)
    cols = lax.broadcasted_iota(jnp.int32, (ts, ts), 1)
    _, l, acc = _flash_update(q, kn_ref[...], vn_ref[...], bn_ref[...], cols <= rows, carry)
    o_ref[...] = (acc / l).astype(BF16)


def _band_sample_kernel(q_ref, kn_ref, vn_ref, kc_ref, vc_ref, bias_ref, o_ref, *, ts, lb):
    q = q_ref[...]
    carry = _flash_update(q, kc_ref[...].astype(BF16), vc_ref[...].astype(BF16), bias_ref[:, :lb], None,
                          _flash_init(ts))
    _, l, acc = _flash_update(q, kn_ref[...], vn_ref[...], bias_ref[:, lb:], None, carry)
    o_ref[...] = (acc / l).astype(BF16)


def _diff_sample_kernel(q_ref, kn_ref, vn_ref, kc_ref, vc_ref, lam_ref, g_ref, li_ref, o_ref, *, ts):
    q2 = _stack_diff_q(q_ref[...])
    carry = _flash_update(q2, kc_ref[...].astype(BF16), vc_ref[...].astype(BF16), None, None, _flash_init(2 * ts))
    carry = _flash_update(q2, kn_ref[...], vn_ref[...], None, None, carry)
    o_ref[...] = _diff_finish(carry, ts, lam_ref, li_ref, g_ref).astype(BF16)


def _sb_sample_kernel(q_ref, kn_ref, vn_ref, kc_ref, vc_ref, tri_ref, o_ref, *, ts):
    q = q_ref[...]
    tri2 = tri_ref[...]
    run = jnp.zeros((ts, LANES), F32)
    acc = jnp.zeros((ts, HEAD_DIM), F32)
    pad = jnp.zeros((LANES - ts, HEAD_DIM), BF16)
    rows = lax.broadcasted_iota(jnp.int32, (ts, LANES), 0)
    cols = lax.broadcasted_iota(jnp.int32, (ts, LANES), 1)
    run, acc = _sb_span(q, jnp.concatenate([kn_ref[...], pad], axis=0), jnp.concatenate([vn_ref[...], pad], axis=0),
                        tri2, cols < rows, run, acc)
    _, acc = _sb_span(q, kc_ref[...].astype(BF16), vc_ref[...].astype(BF16), tri2, None, run, acc)
    o_ref[...] = acc.astype(BF16)


def _sample_call(kern, qkv, group, cache_k, cache_v, layer, bs, ts, extra_in, extra_specs, name):
    m = qkv.shape[0]
    rows = cache_k.shape[1]
    new = lambda comp: pl.BlockSpec((ts, LANES), lambda b, h: (b, _col(group, comp, h)))
    cache = pl.BlockSpec((None, rows, LANES), lambda b, h: (layer * bs + b, 0, h))
    return pl.pallas_call(
        kern,
        grid=(bs, HEADS),
        in_specs=[new(0), new(1), new(2), cache, cache] + extra_specs,
        out_specs=pl.BlockSpec((ts, LANES), lambda b, h: (b, h)),
        out_shape=jax.ShapeDtypeStruct((m, GROUP_W), BF16),
        compiler_params=_cparams("parallel", "arbitrary"),
        name=name,
    )(qkv, qkv, qkv, cache_k, cache_v, *extra_in)


def _rope_tables(pos):
    inv = ROPE_THETA ** (-jnp.arange(ROPE_HALF, dtype=F32) / ROPE_HALF)
    ang = pos.astype(F32)[:, None] * inv[None, :]
    cos, sin = jnp.cos(ang), jnp.sin(ang)
    return jnp.tile(cos, (1, 4)), jnp.concatenate([-sin, sin, -sin, sin], axis=1)


def _band_table(rel, nrows, ncols, q_off):
    period = nrows + ncols
    off = np.arange(period)
    off = np.where(off < ncols, off, off - period)
    vec = rel.astype(F32)[:, :, np.clip(q_off - off, -REL_CLIP, REL_CLIP) + REL_CLIP] * LOG2E
    toep = jnp.tile(vec, (1, 1, nrows))[:, :, :nrows * (period - 1)]
    toep = toep.reshape(rel.shape[0], rel.shape[1], nrows, period - 1)[:, :, :, :ncols]
    qc = (q_off + np.arange(nrows)[:, None]) // CHUNK
    kc = np.arange(ncols)[None, :] // CHUNK
    vis = (kc <= qc) & (kc >= qc - BAND_CHUNKS)
    return jnp.where(jnp.asarray(vis), toep, NEG_INF)


def _row_tile(m, want):
    t = min(m, want)
    assert m % t == 0
    return t


def kernel(x_prompt, x_sample, cache_a_k, cache_a_v, cache_a_logf, cache_b_k, cache_b_v, cache_c_k, cache_c_v, cache_d_k, cache_d_v, attn_norm_g, w_in, fox_b, band_rel_bias, diff_lambda, diff_norm_g, w_out, mlp_norm_g, w_up, w_down, final_norm_g):
    b, seq, d = x_prompt.shape
    bs, ts, _ = x_sample.shape
    depth = w_in.shape[0]
    past = cache_a_k.shape[2]
    lb = cache_b_k.shape[2]
    lbp = min(BAND_PAST, seq)
    mp, ms = b * seq, bs * ts
    assert w_in.shape[2] == QKV_W + HEADS and d == N_GROUPS * GROUP_W
    assert ts == CHUNK and past % LANES == 0 and past % CHUNK == 0 and lb == BAND_PAST

    tm_p = _row_tile(seq, 512)
    tm_s = _row_tile(ms, 512)
    assert tm_s % ts == 0 and tm_p >= lbp
    t_attn = _row_tile(seq, 512)
    assert t_attn >= BAND_PAST

    w_qkv = w_in[:, :, :QKV_W].astype(BF16)
    w_logit = jnp.pad(w_in[:, :, QKV_W:], ((0, 0), (0, 0), (0, LANES - HEADS))).astype(BF16)
    fox_b_pad = jnp.pad(fox_b.astype(F32), ((0, 0), (0, LANES - HEADS)))[:, None, :]
    w_out_b, w_up_b, w_down_b = w_out.astype(BF16), w_up.astype(BF16), w_down.astype(BF16)
    attn_g = attn_norm_g.astype(F32)[:, None, :]
    mlp_g = mlp_norm_g.astype(F32)[:, None, :]
    dng = diff_norm_g.astype(F32)[:, None, :]
    lam_rows = diff_lambda.astype(F32)
    cos_p, sin_p = _rope_tables(jnp.arange(seq))
    cos_s, sin_s = _rope_tables(past + jnp.arange(ts))
    cos_s, sin_s = jnp.tile(cos_s, (tm_s // ts, 1)), jnp.tile(sin_s, (tm_s // ts, 1))
    table_p = _band_table(band_rel_bias, t_attn, 2 * t_attn, t_attn)
    table_s = _band_table(band_rel_bias, ts, lb + ts, lb)
    tri2 = _tri2()

    flat = lambda c: c.reshape(depth * bs, c.shape[2], GROUP_W)
    ca_k, ca_v, cb_k, cb_v = flat(cache_a_k), flat(cache_a_v), flat(cache_b_k), flat(cache_b_v)
    cc_k, cc_v, cd_k, cd_v = flat(cache_c_k), flat(cache_c_v), flat(cache_d_k), flat(cache_d_v)

    xp = x_prompt.reshape(mp, d)
    xs = x_sample.reshape(ms, d)
    p_st = s_st = None
    p_logf, s_logf, s_bk, s_bv = [], [], [], []
    for l in range(depth):
        lam_init = 0.8 - 0.6 * float(np.exp(-0.3 * l))
        li = jnp.asarray([lam_init], F32)

        qkv, logf, p_st = _proj(xp, attn_g, w_qkv, w_logit, fox_b_pad, cos_p, sin_p, l, depth, tm_p, lbp, p_st)
        logf = logf[:, :HEADS].reshape(b, seq, HEADS)
        p_logf.append(logf)
        key_bias = jnp.cumsum(logf, axis=1) * (-LOG2E)
        oa = _fox_prompt(qkv, key_bias, b, seq, t_attn)
        ob = _band_prompt(qkv, table_p, l, b, seq, t_attn)
        oc = _diff_prompt(qkv, lam_rows, dng, li, l, b, seq, t_attn)
        od = _sb_prompt(qkv, tri2, b, seq, t_attn)
        xp = _resid_mm(xp, [oa, ob, oc, od], w_out_b, l, tm_p, 1024)
        u = _up(xp, mlp_g, w_up_b, l, tm_p, 1024)
        xp = _resid_mm(xp, [u], w_down_b, l, tm_p, 512)

        qkv, logf, s_st = _proj(xs, attn_g, w_qkv, w_logit, fox_b_pad, cos_s, sin_s, l, depth, tm_s, 0, s_st)
        logf = logf[:, :HEADS].reshape(bs, ts, HEADS)
        s_logf.append(logf)
        s_bk.append(jnp.concatenate([cache_b_k[l][:, ts:], s_st[2].reshape(bs, ts, HEADS, HEAD_DIM)], axis=1))
        s_bv.append(jnp.concatenate([cache_b_v[l][:, ts:], s_st[3].reshape(bs, ts, HEADS, HEAD_DIM)], axis=1))
        cum_f = jnp.cumsum(jnp.concatenate([cache_a_logf[l].astype(F32), logf], axis=1), axis=1)
        kb = jnp.swapaxes(cum_f * (-LOG2E), 1, 2).reshape(bs * HEADS, 1, past + ts)
        kb_specs = [pl.BlockSpec((None, 1, past), lambda bi, h: (bi * HEADS + h, 0, 0)),
                    pl.BlockSpec((None, 1, ts), lambda bi, h: (bi * HEADS + h, 0, 0))]
        oa = _sample_call(functools.partial(_fox_sample_kernel, ts=ts), qkv, 0, ca_k, ca_v, l, bs, ts,
                          [kb[:, :, :past], kb[:, :, past:]], kb_specs, "fox_sample")
        ob = _sample_call(functools.partial(_band_sample_kernel, ts=ts, lb=lb), qkv, 1, cb_k, cb_v,
                          l, bs, ts, [table_s],
                          [pl.BlockSpec((None, None, ts, lb + ts), lambda bi, h: (l, h, 0, 0))], "band_sample")
        oc = _sample_call(functools.partial(_diff_sample_kernel, ts=ts), qkv, 2, cc_k, cc_v, l, bs, ts,
                          [lam_rows, dng, li],
                          [pl.BlockSpec((None, 4, DIFF_HALF), lambda bi, h: (l, 0, 0)),
                           pl.BlockSpec((None, 1, HEAD_DIM), lambda bi, h: (l, 0, 0)),
                           pl.BlockSpec(memory_space=pltpu.SMEM)], "diff_sample")
        od = _sample_call(functools.partial(_sb_sample_kernel, ts=ts), qkv, 3, cd_k, cd_v,
                          l, bs, ts, [tri2], [pl.BlockSpec((2 * LANES, 2 * LANES), lambda bi, h: (0, 0))],
                          "sb_sample")
        xs = _resid_mm(xs, [oa, ob, oc, od], w_out_b, l, tm_s, 1024)
        u = _up(xs, mlp_g, w_up_b, l, tm_s, 1024)
        xs = _resid_mm(xs, [u], w_down_b, l, tm_s, 512)

    fin_g = final_norm_g.astype(F32)[None, :]
    y_prompt = _final_norm(xp, fin_g, tm_p).reshape(b, seq, d)
    y_sample = _final_norm(xs, fin_g, tm_s).reshape(bs, ts, d)
    pst = lambda n, rows: p_st[n].reshape(depth, b, rows, HEADS, HEAD_DIM)
    sst = lambda n: s_st[n].reshape(depth, bs, ts, HEADS, HEAD_DIM)
    return (y_prompt, y_sample,
            pst(0, seq), pst(1, seq), jnp.stack(p_logf), pst(2, lbp), pst(3, lbp),
            pst(4, seq), pst(5, seq), pst(6, seq), pst(7, seq),
            sst(0), sst(1), jnp.stack(s_logf), jnp.stack(s_bk), jnp.stack(s_bv),
            sst(4), sst(5), sst(6), sst(7))
```

```python
import functools
import math

import numpy as np
import jax
import jax.numpy as jnp
from jax import lax
from jax.experimental import pallas as pl
from jax.experimental.pallas import tpu as pltpu

F32 = jnp.float32
BF16 = jnp.bfloat16

LANES = 128
HEAD_DIM = 128
HEADS = 4
GROUP_W = HEADS * HEAD_DIM
N_GROUPS = 4
QKV_W = 3 * N_GROUPS * GROUP_W
CHUNK = 64
BAND_CHUNKS = 8
BAND_PAST = BAND_CHUNKS * CHUNK
REL_CLIP = 128
DIFF_HALF = HEAD_DIM // 2
ROPE_HALF = DIFF_HALF // 2
ROPE_THETA = 10000.0
EPS = 1e-6
NEG_INF = -1e30
LOG2E = math.log2(math.e)
SCALE = HEAD_DIM ** -0.5
VMEM_LIMIT = 48 * 1024 * 1024

EXP2_ZERO = -160.0

TILE_CQ = 2 * 3 + 0
TILE_CK = 2 * 3 + 1
Q_SCALE = SCALE * LOG2E
Q_SCALE_DIFF = DIFF_HALF ** -0.5 * LOG2E


def _cparams(*sem):
    return pltpu.CompilerParams(dimension_semantics=sem, vmem_limit_bytes=VMEM_LIMIT)


def _dot(a, b):
    return jnp.dot(a, b, preferred_element_type=F32)


def _dot_nt(a, b):
    return lax.dot_general(a, b, (((1,), (1,)), ((), ())), preferred_element_type=F32)


def _rmsnorm_rows(x, g):
    return x * lax.rsqrt(jnp.mean(x * x, axis=-1, keepdims=True) + EPS) * g


def _log_sigmoid(z):
    return jnp.minimum(z, 0.0) - jnp.log1p(jnp.exp(-jnp.abs(z)))


def _row_norm(x):
    xf = x.astype(F32)
    return jnp.sqrt(jnp.sum(xf * xf, axis=-1, keepdims=True))


def _rope_tile(y, cos, sin):
    lane = lax.broadcasted_iota(jnp.int32, (1, LANES), 1)
    first = (lane % DIFF_HALF) < ROPE_HALF
    outs = []
    for h in range(HEADS):
        x = y[:, h * LANES:(h + 1) * LANES]
        partner = jnp.where(first, pltpu.roll(x, LANES - ROPE_HALF, 1), pltpu.roll(x, ROPE_HALF, 1))
        outs.append(x * cos + partner * sin)
    return jnp.concatenate(outs, axis=1)


def _store_heads(st_ref, val):
    rows = val.shape[0]
    for h in range(HEADS):
        st_ref[pl.ds(h, rows, stride=HEADS), :] = val[:, h * LANES:(h + 1) * LANES]


def _proj_kernel(*refs, n_alias, nper, tail_rows):
    x_ref, g_ref, w_ref, wl_ref, fb_ref, cos_ref, sin_ref = refs[:7]
    qkv_ref, logf_ref = refs[7 + n_alias:9 + n_alias]
    st_refs = refs[9 + n_alias:17 + n_alias]
    h_ref = refs[17 + n_alias]
    i, j = pl.program_id(0), pl.program_id(1)
    tm = x_ref.shape[0]

    @pl.when(j == 0)
    def _():
        hb = _rmsnorm_rows(x_ref[...], g_ref[...]).astype(BF16)
        h_ref[...] = hb
        logf_ref[...] = _log_sigmoid(_dot(hb, wl_ref[...]) + fb_ref[...])

    acc = _dot(h_ref[...], w_ref[...])
    is_q = j % 3 == 0

    @pl.when(j == TILE_CQ)
    def _():
        qkv_ref[...] = (_rope_tile(acc, cos_ref[...], sin_ref[...]) * Q_SCALE_DIFF).astype(BF16)

    @pl.when(j == TILE_CK)
    def _():
        y = _rope_tile(acc, cos_ref[...], sin_ref[...])
        qkv_ref[...] = y.astype(BF16)
        _store_heads(st_refs[4], y)

    @pl.when(jnp.logical_and(is_q, j != TILE_CQ))
    def _():
        qkv_ref[...] = (acc * Q_SCALE).astype(BF16)

    @pl.when(jnp.logical_and(jnp.logical_not(is_q), j != TILE_CK))
    def _():
        qkv_ref[...] = acc.astype(BF16)

    for n, st_ref in enumerate(st_refs):
        if n == 4:
            continue
        cond = j == 3 * (n // 2) + 1 + n % 2
        if tail_rows and n in (2, 3):
            @pl.when(jnp.logical_and(cond, i % nper == nper - 1))
            def _(st_ref=st_ref):
                _store_heads(st_ref, acc[tm - tail_rows:, :])
        else:
            @pl.when(cond)
            def _(st_ref=st_ref):
                _store_heads(st_ref, acc)


def _proj(x, g, w_qkv, w_logit, fox_b, cos_t, sin_t, layer, depth, tm, tail_rows, bufs):
    m, d = x.shape
    tn = GROUP_W
    nt = m // tm
    nper = cos_t.shape[0] // tm
    full = pl.BlockSpec((tm * HEADS, LANES), lambda i, j: (layer * nt + i, 0))
    full_shape = jax.ShapeDtypeStruct((depth * m * HEADS, LANES), F32)
    if tail_rows:
        nstream = nt // nper
        bspec = pl.BlockSpec((tail_rows * HEADS, LANES), lambda i, j: (layer * nstream + i // nper, 0))
        bshape = jax.ShapeDtypeStruct((depth * nstream * tail_rows * HEADS, LANES), F32)
    else:
        bspec = pl.BlockSpec((tm * HEADS, LANES), lambda i, j: (i, 0))
        bshape = jax.ShapeDtypeStruct((m * HEADS, LANES), F32)
    st_specs = [full, full, bspec, bspec, full, full, full, full]
    st_shapes = [full_shape, full_shape, bshape, bshape, full_shape, full_shape, full_shape, full_shape]
    stacked = [n for n in range(8) if tail_rows or n not in (2, 3)]
    alias_in = [bufs[n] for n in stacked] if bufs is not None else []
    aliases = {7 + k: 2 + n for k, n in enumerate(stacked)} if bufs is not None else {}
    outs = pl.pallas_call(
        functools.partial(_proj_kernel, n_alias=len(alias_in), nper=nper, tail_rows=tail_rows),
        grid=(nt, QKV_W // tn),
        in_specs=[
            pl.BlockSpec((tm, d), lambda i, j: (i, 0)),
            pl.BlockSpec((None, 1, d), lambda i, j: (layer, 0, 0)),
            pl.BlockSpec((None, d, tn), lambda i, j: (layer, 0, j)),
            pl.BlockSpec((None, d, LANES), lambda i, j: (layer, 0, 0)),
            pl.BlockSpec((None, 1, LANES), lambda i, j: (layer, 0, 0)),
            pl.BlockSpec((tm, LANES), lambda i, j: (i % nper, 0)),
            pl.BlockSpec((tm, LANES), lambda i, j: (i % nper, 0)),
        ] + [pl.BlockSpec(memory_space=pl.ANY)] * len(alias_in),
        out_specs=[
            pl.BlockSpec((tm, tn), lambda i, j: (i, j)),
            pl.BlockSpec((tm, LANES), lambda i, j: (i, 0)),
        ] + st_specs,
        out_shape=[
            jax.ShapeDtypeStruct((m, QKV_W), BF16),
            jax.ShapeDtypeStruct((m, LANES), F32),
        ] + st_shapes,
        scratch_shapes=[pltpu.VMEM((tm, d), BF16)],
        input_output_aliases=aliases,
        compiler_params=_cparams("parallel", "arbitrary"),
        name="proj",
    )(x, g, w_qkv, w_logit, fox_b, cos_t, sin_t, *alias_in)
    return outs[0], outs[1], list(outs[2:])


def _resid_mm_kernel(*refs, widths):
    n_in = len(widths)
    x_ref, a_refs, w_ref, o_ref = refs[0], refs[1:1 + n_in], refs[1 + n_in], refs[2 + n_in]
    acc = x_ref[...]
    off = 0
    for a_ref, wd in zip(a_refs, widths):
        acc = acc + _dot(a_ref[...], w_ref[off:off + wd, :])
        off += wd
    o_ref[...] = acc


def _resid_mm(x, a_list, w, layer, tm, tn):
    m, n = x.shape
    widths = tuple(a.shape[1] for a in a_list)
    k = sum(widths)
    in_specs = [pl.BlockSpec((tm, tn), lambda i, j: (i, j))]
    in_specs += [pl.BlockSpec((tm, wd), lambda i, j: (i, 0)) for wd in widths]
    in_specs += [pl.BlockSpec((None, k, tn), lambda i, j: (layer, 0, j))]
    return pl.pallas_call(
        functools.partial(_resid_mm_kernel, widths=widths),
        grid=(m // tm, n // tn),
        in_specs=in_specs,
        out_specs=pl.BlockSpec((tm, tn), lambda i, j: (i, j)),
        out_shape=jax.ShapeDtypeStruct((m, n), F32),
        compiler_params=_cparams("parallel", "arbitrary"),
        name="resid_mm",
    )(x, *a_list, w)


def _up_kernel(x_ref, g_ref, w_ref, u_ref, h_ref):
    @pl.when(pl.program_id(1) == 0)
    def _():
        h_ref[...] = _rmsnorm_rows(x_ref[...], g_ref[...]).astype(BF16)

    r = jnp.maximum(_dot(h_ref[...], w_ref[...]), 0.0)
    u_ref[...] = (r * r).astype(BF16)


def _up(x, g, w_up, layer, tm, tn):
    m, d = x.shape
    f = w_up.shape[2]
    return pl.pallas_call(
        _up_kernel,
        grid=(m // tm, f // tn),
        in_specs=[
            pl.BlockSpec((tm, d), lambda i, j: (i, 0)),
            pl.BlockSpec((None, 1, d), lambda i, j: (layer, 0, 0)),
            pl.BlockSpec((None, d, tn), lambda i, j: (layer, 0, j)),
        ],
        out_specs=pl.BlockSpec((tm, tn), lambda i, j: (i, j)),
        out_shape=jax.ShapeDtypeStruct((m, f), BF16),
        scratch_shapes=[pltpu.VMEM((tm, d), BF16)],
        compiler_params=_cparams("parallel", "arbitrary"),
        name="mlp_up",
    )(x, g, w_up)


def _final_norm_kernel(x_ref, g_ref, o_ref):
    o_ref[...] = _rmsnorm_rows(x_ref[...], g_ref[...])


def _final_norm(x, g, tm):
    m, d = x.shape
    return pl.pallas_call(
        _final_norm_kernel,
        grid=(m // tm,),
        in_specs=[pl.BlockSpec((tm, d), lambda i: (i, 0)), pl.BlockSpec((1, d), lambda i: (0, 0))],
        out_specs=pl.BlockSpec((tm, d), lambda i: (i, 0)),
        out_shape=jax.ShapeDtypeStruct((m, d), F32),
        compiler_params=_cparams("parallel"),
        name="final_norm",
    )(x, g)


def _flash_init(rows):
    return (jnp.full((rows, 1), NEG_INF, F32), jnp.zeros((rows, 1), F32), jnp.zeros((rows, HEAD_DIM), F32))


def _flash_update(q, k, v, bias, mask, carry):
    m, l, acc = carry
    s = _dot_nt(q, k)
    if bias is not None:
        s = s + bias
    if mask is not None:
        s = jnp.where(mask, s, NEG_INF)
    m_new = jnp.maximum(m, jnp.max(s, axis=-1, keepdims=True))
    alpha = jnp.exp2(m - m_new)
    p = jnp.exp2(s - m_new)
    l = alpha * l + jnp.sum(p, axis=-1, keepdims=True)
    acc = alpha * acc + _dot(p.astype(BF16), v)
    return m_new, l, acc


def _stack_diff_q(q):
    lane = lax.broadcasted_iota(jnp.int32, q.shape, 1)
    zero = jnp.zeros_like(q)
    return jnp.concatenate([jnp.where(lane < DIFF_HALF, q, zero), jnp.where(lane >= DIFF_HALF, q, zero)], axis=0)


def _diff_finish(carry, tq, lam_ref, li_ref, g_ref):
    _, l, acc = carry
    o = acc / l
    lv = lam_ref[...]
    lam = (jnp.exp(jnp.sum(lv[0:1] * lv[1:2], axis=-1, keepdims=True))
           - jnp.exp(jnp.sum(lv[2:3] * lv[3:4], axis=-1, keepdims=True)) + li_ref[0])
    d = o[:tq] - lam * o[tq:]
    return _rmsnorm_rows(d, g_ref[...]) * (1.0 - li_ref[0])


def _sb_span(q, k, v, tri2, mask, run):
    nsub = k.shape[0] // LANES
    z = _dot_nt(q, k)
    neg_abs = pltpu.bitcast(pltpu.bitcast(z, jnp.uint32) | jnp.uint32(0x80000000), F32)
    sp = jnp.maximum(z, 0.0) + jnp.log2(1.0 + jnp.exp2(neg_abs))
    if mask is not None:
        sp = jnp.where(mask, sp, 0.0)
    hi = sp.astype(BF16)
    lo = (sp - hi.astype(F32)).astype(BF16)
    parts = [None] * nsub
    for n in reversed(range(nsub)):
        c = slice(n * LANES, (n + 1) * LANES)
        sums = _dot(jnp.concatenate([hi[:, c], lo[:, c]], axis=1), tri2)
        parts[n] = z[:, c] + sums[:, :LANES] + run
        run = run + sums[:, LANES:]
    a = jnp.exp2(jnp.concatenate(parts, axis=1))
    if mask is not None:
        a = jnp.where(mask, a, 0.0)
    return run, _dot(a.astype(BF16), v)


def _tri2():
    j = np.arange(2 * LANES)[:, None] % LANES
    s = np.arange(2 * LANES)[None, :]
    return jnp.asarray(np.where(np.where(s < LANES, j >= s, True), -1.0, 0.0), dtype=BF16)


def _key_norm_max(k_ref, kmax_ref):
    @pl.when(pl.program_id(2) == 0)
    def _():
        kmax_ref[...] = jnp.max(_row_norm(k_ref[...]), axis=0, keepdims=True)


def _fox_prompt_kernel(q_ref, k_ref, v_ref, b_ref, o_ref, kmax_ref, *, t):
    i = pl.program_id(2)
    _key_norm_max(k_ref, kmax_ref)
    q = q_ref[...]
    bound = _row_norm(q) * kmax_ref[...]

    def update(j, carry, mask):
        ks = pl.multiple_of(j * t, t)
        return _flash_update(q, k_ref[pl.ds(ks, t), :], v_ref[pl.ds(ks, t), :], b_ref[j], mask, carry)

    def visible(j, m):
        top = jnp.max(b_ref[jnp.maximum(j, 0)], axis=-1, keepdims=True)
        return jnp.max(bound + top - m) > EXP2_ZERO

    rows = lax.broadcasted_iota(jnp.int32, (t, t), 0)
    cols = lax.broadcasted_iota(jnp.int32, (t, t), 1)
    carry = update(i, _flash_init(t), cols <= rows)

    def body(c):
        carry = update(c[0], c[2:], None)
        return (c[0] - 1, visible(c[0] - 1, carry[0])) + carry

    c = lax.while_loop(lambda c: jnp.logical_and(c[0] >= 0, c[1]), body, (i - 1, visible(i - 1, carry[0])) + carry)
    o_ref[...] = (c[4] / c[3]).astype(BF16)


def _diff_prompt_kernel(q_ref, k_ref, v_ref, lam_ref, g_ref, li_ref, o_ref, *, tq, tk):
    i = pl.program_id(2)
    q2 = _stack_diff_q(q_ref[...])
    jd = (i * tq) // tk

    def step(j, carry, mask):
        ks = pl.multiple_of(j * tk, tk)
        return _flash_update(q2, k_ref[pl.ds(ks, tk), :], v_ref[pl.ds(ks, tk), :], None, mask, carry)

    carry = lax.fori_loop(0, jd, lambda j, c: step(j, c, None), _flash_init(2 * tq))
    rows = lax.broadcasted_iota(jnp.int32, (2 * tq, tk), 0) % tq + i * tq
    cols = lax.broadcasted_iota(jnp.int32, (2 * tq, tk), 1) + jd * tk
    carry = step(jd, carry, (cols // CHUNK) <= (rows // CHUNK))
    o_ref[...] = _diff_finish(carry, tq, lam_ref, li_ref, g_ref).astype(BF16)


def _sb_prompt_kernel(q_ref, k_ref, v_ref, tri_ref, o_ref, kmax_ref, run_ref, acc_ref, *, t):
    i = pl.program_id(2)
    _key_norm_max(k_ref, kmax_ref)
    q = q_ref[...]
    tri2 = tri_ref[...]
    bound = _row_norm(q) * kmax_ref[...]
    rows = lax.broadcasted_iota(jnp.int32, (t, t), 0)
    cols = lax.broadcasted_iota(jnp.int32, (t, t), 1)

    def span(j, mask, run):
        ks = pl.multiple_of(j * t, t)
        run, pv = _sb_span(q, k_ref[pl.ds(ks, t), :], v_ref[pl.ds(ks, t), :], tri2, mask, run)
        run_ref[...] = run
        return pv

    def visible():
        return jnp.max(bound + run_ref[:, :1]) > EXP2_ZERO

    acc_ref[...] = span(i, cols < rows, jnp.zeros((t, LANES), F32))

    def body(c):
        acc_ref[...] += span(c[0], None, run_ref[...])
        return c[0] - 1, visible()

    lax.while_loop(lambda c: jnp.logical_and(c[0] >= 0, c[1]), body, (i - 1, visible()))
    o_ref[...] = acc_ref[...].astype(BF16)


def _band_prompt_kernel(q_ref, kp_ref, kc_ref, vp_ref, vc_ref, bias_ref, o_ref, *, t):
    i = pl.program_id(2)
    q = q_ref[...]
    sp = _dot_nt(q, kp_ref[...]) + bias_ref[:, :t]
    sp = jnp.where(i > 0, sp, NEG_INF)
    sc = _dot_nt(q, kc_ref[...]) + bias_ref[:, t:]
    m = jnp.maximum(jnp.max(sp, axis=-1, keepdims=True), jnp.max(sc, axis=-1, keepdims=True))
    pp = jnp.exp2(sp - m)
    pc = jnp.exp2(sc - m)
    l = jnp.sum(pp, axis=-1, keepdims=True) + jnp.sum(pc, axis=-1, keepdims=True)
    o = _dot(pp.astype(BF16), vp_ref[...]) + _dot(pc.astype(BF16), vc_ref[...])
    o_ref[...] = (o / l).astype(BF16)


def _col(group, comp, h):
    return group * 3 * HEADS + comp * HEADS + h


def _prompt_call(kern, qkv, group, b, seq, tq, extra_in, extra_specs, scratch, name):
    nq = seq // tq
    m = qkv.shape[0]
    in_specs = [
        pl.BlockSpec((tq, LANES), lambda bi, h, i: (bi * nq + i, _col(group, 0, h))),
        pl.BlockSpec((seq, LANES), lambda bi, h, i: (bi, _col(group, 1, h))),
        pl.BlockSpec((seq, LANES), lambda bi, h, i: (bi, _col(group, 2, h))),
    ] + extra_specs
    return pl.pallas_call(
        kern,
        grid=(b, HEADS, nq),
        in_specs=in_specs,
        out_specs=pl.BlockSpec((tq, LANES), lambda bi, h, i: (bi * nq + i, h)),
        out_shape=jax.ShapeDtypeStruct((m, GROUP_W), BF16),
        scratch_shapes=scratch,
        compiler_params=_cparams("parallel", "parallel", "arbitrary"),
        name=name,
    )(qkv, qkv, qkv, *extra_in)


def _fox_prompt(qkv, key_bias, b, seq, t):
    nk = seq // t
    kb = jnp.swapaxes(key_bias, 1, 2).reshape(b * HEADS, nk, 1, t)
    spec = pl.BlockSpec((None, nk, 1, t), lambda bi, h, i: (bi * HEADS + h, 0, 0, 0))
    kern = functools.partial(_fox_prompt_kernel, t=t)
    return _prompt_call(kern, qkv, 0, b, seq, t, [kb], [spec], [pltpu.VMEM((1, 1), F32)], "fox_prompt")


def _diff_prompt(qkv, lam, dng, li, layer, b, seq, tq, tk):
    specs = [
        pl.BlockSpec((None, 4, DIFF_HALF), lambda bi, h, i: (layer, 0, 0)),
        pl.BlockSpec((None, 1, HEAD_DIM), lambda bi, h, i: (layer, 0, 0)),
        pl.BlockSpec(memory_space=pltpu.SMEM),
    ]
    kern = functools.partial(_diff_prompt_kernel, tq=tq, tk=tk)
    return _prompt_call(kern, qkv, 2, b, seq, tq, [lam, dng, li], specs, [], "diff_prompt")


def _sb_prompt(qkv, tri2, b, seq, t):
    spec = pl.BlockSpec((2 * LANES, 2 * LANES), lambda bi, h, i: (0, 0))
    kern = functools.partial(_sb_prompt_kernel, t=t)
    scratch = [pltpu.VMEM((1, 1), F32), pltpu.VMEM((t, LANES), F32), pltpu.VMEM((t, HEAD_DIM), F32)]
    return _prompt_call(kern, qkv, 3, b, seq, t, [tri2], [spec], scratch, "sb_prompt")


def _band_prompt(qkv, table, layer, b, seq, t):
    nq = seq // t
    m = qkv.shape[0]
    cur = lambda comp: (lambda bi, h, i: (bi * nq + i, _col(1, comp, h)))
    prev = lambda comp: (lambda bi, h, i: (bi * nq + jnp.maximum(i - 1, 0), _col(1, comp, h)))
    return pl.pallas_call(
        functools.partial(_band_prompt_kernel, t=t),
        grid=(b, HEADS, nq),
        in_specs=[
            pl.BlockSpec((t, LANES), cur(0)),
            pl.BlockSpec((t, LANES), prev(1)),
            pl.BlockSpec((t, LANES), cur(1)),
            pl.BlockSpec((t, LANES), prev(2)),
            pl.BlockSpec((t, LANES), cur(2)),
            pl.BlockSpec((None, None, t, 2 * t), lambda bi, h, i: (layer, h, 0, 0)),
        ],
        out_specs=pl.BlockSpec((t, LANES), lambda bi, h, i: (bi * nq + i, h)),
        out_shape=jax.ShapeDtypeStruct((m, GROUP_W), BF16),
        compiler_params=_cparams("parallel", "parallel", "arbitrary"),
        name="band_prompt",
    )(qkv, qkv, qkv, qkv, qkv, table)


def _cached_head(c_ref):
    rows = c_ref.shape[0] // HEADS
    return c_ref[pl.ds(pl.program_id(1), rows, stride=HEADS), :].astype(BF16)


def _fox_sample_kernel(q_ref, kn_ref, vn_ref, kc_ref, vc_ref, bc_ref, bn_ref, o_ref, *, ts):
    q = q_ref[...]
    carry = _flash_update(q, _cached_head(kc_ref), _cached_head(vc_ref), bc_ref[...], None, _flash_init(ts))
    rows = lax.broadcasted_iota(jnp.int32, (ts, ts), 0)
    cols = lax.broadcasted_iota(jnp.int32, (ts, ts), 1)
    _, l, acc = _flash_update(q, kn_ref[...], vn_ref[...], bn_ref[...], cols <= rows, carry)
    o_ref[...] = (acc / l).astype(BF16)


def _band_sample_kernel(q_ref, kn_ref, vn_ref, kc_ref, vc_ref, bias_ref, o_ref, *, ts, lb):
    q = q_ref[...]
    carry = _flash_update(q, _cached_head(kc_ref), _cached_head(vc_ref), bias_ref[:, :lb], None,
                          _flash_init(ts))
    _, l, acc = _flash_update(q, kn_ref[...], vn_ref[...], bias_ref[:, lb:], None, carry)
    o_ref[...] = (acc / l).astype(BF16)


def _diff_sample_kernel(q_ref, kn_ref, vn_ref, kc_ref, vc_ref, lam_ref, g_ref, li_ref, o_ref, *, ts):
    q2 = _stack_diff_q(q_ref[...])
    carry = _flash_update(q2, _cached_head(kc_ref), _cached_head(vc_ref), None, None, _flash_init(2 * ts))
    carry = _flash_update(q2, kn_ref[...], vn_ref[...], None, None, carry)
    o_ref[...] = _diff_finish(carry, ts, lam_ref, li_ref, g_ref).astype(BF16)


def _sb_sample_kernel(q_ref, kn_ref, vn_ref, kc_ref, vc_ref, tri_ref, o_ref, *, ts):
    q = q_ref[...]
    tri2 = tri_ref[...]
    pad = jnp.zeros((LANES - ts, HEAD_DIM), BF16)
    rows = lax.broadcasted_iota(jnp.int32, (ts, LANES), 0)
    cols = lax.broadcasted_iota(jnp.int32, (ts, LANES), 1)
    run, acc = _sb_span(q, jnp.concatenate([kn_ref[...], pad], axis=0), jnp.concatenate([vn_ref[...], pad], axis=0),
                        tri2, cols < rows, jnp.zeros((ts, LANES), F32))
    _, pv = _sb_span(q, _cached_head(kc_ref), _cached_head(vc_ref), tri2, None, run)
    o_ref[...] = (acc + pv).astype(BF16)


def _sample_call(kern, qkv, group, cache_k, cache_v, layer, bs, ts, extra_in, extra_specs, name):
    m = qkv.shape[0]
    rows = cache_k.shape[1]
    new = lambda comp: pl.BlockSpec((ts, LANES), lambda b, h: (b, _col(group, comp, h)))
    cache = pl.BlockSpec((None, rows, LANES), lambda b, h: (layer * bs + b, 0, 0))
    return pl.pallas_call(
        kern,
        grid=(bs, HEADS),
        in_specs=[new(0), new(1), new(2), cache, cache] + extra_specs,
        out_specs=pl.BlockSpec((ts, LANES), lambda b, h: (b, h)),
        out_shape=jax.ShapeDtypeStruct((m, GROUP_W), BF16),
        compiler_params=_cparams("parallel", "arbitrary"),
        name=name,
    )(qkv, qkv, qkv, cache_k, cache_v, *extra_in)


def _rope_tables(pos):
    inv = ROPE_THETA ** (-jnp.arange(ROPE_HALF, dtype=F32) / ROPE_HALF)
    ang = pos.astype(F32)[:, None] * inv[None, :]
    cos, sin = jnp.cos(ang), jnp.sin(ang)
    return jnp.tile(cos, (1, 4)), jnp.concatenate([-sin, sin, -sin, sin], axis=1)


def _band_table(rel, nrows, ncols, q_off):
    period = nrows + ncols
    off = np.arange(period)
    off = np.where(off < ncols, off, off - period)
    vec = rel.astype(F32)[:, :, np.clip(q_off - off, -REL_CLIP, REL_CLIP) + REL_CLIP] * LOG2E
    toep = jnp.tile(vec, (1, 1, nrows))[:, :, :nrows * (period - 1)]
    toep = toep.reshape(rel.shape[0], rel.shape[1], nrows, period - 1)[:, :, :, :ncols]
    qc = (q_off + np.arange(nrows)[:, None]) // CHUNK
    kc = np.arange(ncols)[None, :] // CHUNK
    vis = (kc <= qc) & (kc >= qc - BAND_CHUNKS)
    return jnp.where(jnp.asarray(vis), toep, NEG_INF)


def _row_tile(m, want):
    t = min(m, want)
    assert m % t == 0
    return t


def kernel(x_prompt, x_sample, cache_a_k, cache_a_v, cache_a_logf, cache_b_k, cache_b_v, cache_c_k, cache_c_v, cache_d_k, cache_d_v, attn_norm_g, w_in, fox_b, band_rel_bias, diff_lambda, diff_norm_g, w_out, mlp_norm_g, w_up, w_down, final_norm_g):
    b, seq, d = x_prompt.shape
    bs, ts, _ = x_sample.shape
    depth = w_in.shape[0]
    past = cache_a_k.shape[2]
    lb = cache_b_k.shape[2]
    lbp = min(BAND_PAST, seq)
    mp, ms = b * seq, bs * ts
    assert w_in.shape[2] == QKV_W + HEADS and d == N_GROUPS * GROUP_W
    assert ts == CHUNK and past % LANES == 0 and past % CHUNK == 0 and lb == BAND_PAST

    tm_p = _row_tile(seq, 512)
    tm_s = _row_tile(ms, 512)
    assert tm_s % ts == 0 and tm_p >= lbp
    t_attn = _row_tile(seq, 512)
    t_diff_k = _row_tile(seq, 1024)
    assert t_attn >= BAND_PAST and t_diff_k % t_attn == 0

    w_qkv = w_in[:, :, :QKV_W].astype(BF16)
    w_logit = jnp.pad(w_in[:, :, QKV_W:], ((0, 0), (0, 0), (0, LANES - HEADS))).astype(BF16)
    fox_b_pad = jnp.pad(fox_b.astype(F32), ((0, 0), (0, LANES - HEADS)))[:, None, :]
    w_out_b, w_up_b, w_down_b = w_out.astype(BF16), w_up.astype(BF16), w_down.astype(BF16)
    attn_g = attn_norm_g.astype(F32)[:, None, :]
    mlp_g = mlp_norm_g.astype(F32)[:, None, :]
    dng = diff_norm_g.astype(F32)[:, None, :]
    lam_rows = diff_lambda.astype(F32)
    cos_p, sin_p = _rope_tables(jnp.arange(seq))
    cos_s, sin_s = _rope_tables(past + jnp.arange(ts))
    cos_s, sin_s = jnp.tile(cos_s, (tm_s // ts, 1)), jnp.tile(sin_s, (tm_s // ts, 1))
    table_p = _band_table(band_rel_bias, t_attn, 2 * t_attn, t_attn)
    table_s = _band_table(band_rel_bias, ts, lb + ts, lb)
    tri2 = _tri2()

    flat = lambda c: c.reshape(depth * bs, c.shape[2] * HEADS, HEAD_DIM)
    ca_k, ca_v, cb_k, cb_v = flat(cache_a_k), flat(cache_a_v), flat(cache_b_k), flat(cache_b_v)
    cc_k, cc_v, cd_k, cd_v = flat(cache_c_k), flat(cache_c_v), flat(cache_d_k), flat(cache_d_v)

    xp = x_prompt.reshape(mp, d)
    xs = x_sample.reshape(ms, d)
    p_st = s_st = None
    p_logf, s_logf, s_bk, s_bv = [], [], [], []
    for l in range(depth):
        lam_init = 0.8 - 0.6 * float(np.exp(-0.3 * l))
        li = jnp.asarray([lam_init], F32)

        qkv, logf, p_st = _proj(xp, attn_g, w_qkv, w_logit, fox_b_pad, cos_p, sin_p, l, depth, tm_p, lbp, p_st)
        logf = logf[:, :HEADS].reshape(b, seq, HEADS)
        p_logf.append(logf)
        key_bias = jnp.cumsum(logf, axis=1) * (-LOG2E)
        oa = _fox_prompt(qkv, key_bias, b, seq, t_attn)
        ob = _band_prompt(qkv, table_p, l, b, seq, t_attn)
        oc = _diff_prompt(qkv, lam_rows, dng, li, l, b, seq, t_attn, t_diff_k)
        od = _sb_prompt(qkv, tri2, b, seq, t_attn)
        xp = _resid_mm(xp, [oa, ob, oc, od], w_out_b, l, tm_p, 1024)
        u = _up(xp, mlp_g, w_up_b, l, tm_p, 1024)
        xp = _resid_mm(xp, [u], w_down_b, l, tm_p, 512)

        qkv, logf, s_st = _proj(xs, attn_g, w_qkv, w_logit, fox_b_pad, cos_s, sin_s, l, depth, tm_s, 0, s_st)
        logf = logf[:, :HEADS].reshape(bs, ts, HEADS)
        s_logf.append(logf)
        s_bk.append(jnp.concatenate([cache_b_k[l][:, ts:], s_st[2].reshape(bs, ts, HEADS, HEAD_DIM)], axis=1))
        s_bv.append(jnp.concatenate([cache_b_v[l][:, ts:], s_st[3].reshape(bs, ts, HEADS, HEAD_DIM)], axis=1))
        cum_f = jnp.cumsum(jnp.concatenate([cache_a_logf[l].astype(F32), logf], axis=1), axis=1)
        kb = jnp.swapaxes(cum_f * (-LOG2E), 1, 2).reshape(bs * HEADS, 1, past + ts)
        kb_specs = [pl.BlockSpec((None, 1, past), lambda bi, h: (bi * HEADS + h, 0, 0)),
                    pl.BlockSpec((None, 1, ts), lambda bi, h: (bi * HEADS + h, 0, 0))]
        oa = _sample_call(functools.partial(_fox_sample_kernel, ts=ts), qkv, 0, ca_k, ca_v, l, bs, ts,
                          [kb[:, :, :past], kb[:, :, past:]], kb_specs, "fox_sample")
        ob = _sample_call(functools.partial(_band_sample_kernel, ts=ts, lb=lb), qkv, 1, cb_k, cb_v,
                          l, bs, ts, [table_s],
                          [pl.BlockSpec((None, None, ts, lb + ts), lambda bi, h: (l, h, 0, 0))], "band_sample")
        oc = _sample_call(functools.partial(_diff_sample_kernel, ts=ts), qkv, 2, cc_k, cc_v, l, bs, ts,
                          [lam_rows, dng, li],
                          [pl.BlockSpec((None, 4, DIFF_HALF), lambda bi, h: (l, 0, 0)),
                           pl.BlockSpec((None, 1, HEAD_DIM), lambda bi, h: (l, 0, 0)),
                           pl.BlockSpec(memory_space=pltpu.SMEM)], "diff_sample")
        od = _sample_call(functools.partial(_sb_sample_kernel, ts=ts), qkv, 3, cd_k, cd_v,
                          l, bs, ts, [tri2], [pl.BlockSpec((2 * LANES, 2 * LANES), lambda bi, h: (0, 0))],
                          "sb_sample")
        xs = _resid_mm(xs, [oa, ob, oc, od], w_out_b, l, tm_s, 1024)
        u = _up(xs, mlp_g, w_up_b, l, tm_s, 1024)
        xs = _resid_mm(xs, [u], w_down_b, l, tm_s, 512)

    fin_g = final_norm_g.astype(F32)[None, :]
    y_prompt = _final_norm(xp, fin_g, tm_p).reshape(b, seq, d)
    y_sample = _final_norm(xs, fin_g, tm_s).reshape(bs, ts, d)
    pst = lambda n, rows: p_st[n].reshape(depth, b, rows, HEADS, HEAD_DIM)
    sst = lambda n: s_st[n].reshape(depth, bs, ts, HEADS, HEAD_DIM)
    return (y_prompt, y_sample,
            pst(0, seq), pst(1, seq), jnp.stack(p_logf), pst(2, lbp), pst(3, lbp),
            pst(4, seq), pst(5, seq), pst(6, seq), pst(7, seq),
            sst(0), sst(1), jnp.stack(s_logf), jnp.stack(s_bk), jnp.stack(s_bv),
            sst(4), sst(5), sst(6), sst(7))
```

```python
import functools
import math

import numpy as np
import jax
import jax.numpy as jnp
from jax import lax
from jax.experimental import pallas as pl
from jax.experimental.pallas import tpu as pltpu

F32 = jnp.float32
BF16 = jnp.bfloat16

LANES = 128
HEAD_DIM = 128
HEADS = 4
GROUP_W = HEADS * HEAD_DIM
N_GROUPS = 4
QKV_W = 3 * N_GROUPS * GROUP_W
CHUNK = 64
BAND_CHUNKS = 8
BAND_PAST = BAND_CHUNKS * CHUNK
REL_CLIP = 128
DIFF_HALF = HEAD_DIM // 2
ROPE_HALF = DIFF_HALF // 2
ROPE_THETA = 10000.0
EPS = 1e-6
NEG_INF = -1e30
LOG2E = math.log2(math.e)
SCALE = HEAD_DIM ** -0.5
VMEM_LIMIT = 56 * 1024 * 1024

EXP2_ZERO = -160.0

TILE_CQ = 2 * 3 + 0
TILE_CK = 2 * 3 + 1
Q_SCALE = SCALE * LOG2E
Q_SCALE_DIFF = DIFF_HALF ** -0.5 * LOG2E


def _cparams(*sem):
    return pltpu.CompilerParams(dimension_semantics=sem, vmem_limit_bytes=VMEM_LIMIT)


def _dot(a, b):
    return jnp.dot(a, b, preferred_element_type=F32)


def _dot_nt(a, b):
    return lax.dot_general(a, b, (((1,), (1,)), ((), ())), preferred_element_type=F32)


def _rmsnorm_rows(x, g):
    return x * lax.rsqrt(jnp.mean(x * x, axis=-1, keepdims=True) + EPS) * g


def _log_sigmoid(z):
    return jnp.minimum(z, 0.0) - jnp.log1p(jnp.exp(-jnp.abs(z)))


def _row_norm(x):
    xf = x.astype(F32)
    return jnp.sqrt(jnp.sum(xf * xf, axis=-1, keepdims=True))


def _rope_tile(y, cos, sin):
    lane = lax.broadcasted_iota(jnp.int32, (1, LANES), 1)
    first = (lane % DIFF_HALF) < ROPE_HALF
    outs = []
    for h in range(HEADS):
        x = y[:, h * LANES:(h + 1) * LANES]
        partner = jnp.where(first, pltpu.roll(x, LANES - ROPE_HALF, 1), pltpu.roll(x, ROPE_HALF, 1))
        outs.append(x * cos + partner * sin)
    return jnp.concatenate(outs, axis=1)


def _store_heads(st_ref, val):
    rows = val.shape[0]
    for h in range(HEADS):
        st_ref[pl.ds(h, rows, stride=HEADS), :] = val[:, h * LANES:(h + 1) * LANES]


def _proj_kernel(*refs, n_alias, nper, tail_rows):
    h_ref, w_ref, wl_ref, fb_ref, cos_ref, sin_ref = refs[:6]
    qkv_ref, logf_ref = refs[6 + n_alias:8 + n_alias]
    st_refs = refs[8 + n_alias:16 + n_alias]
    i, j = pl.program_id(0), pl.program_id(1)
    tm = h_ref.shape[0]

    @pl.when(j == 0)
    def _():
        logf_ref[...] = _log_sigmoid(_dot(h_ref[...], wl_ref[...]) + fb_ref[...])

    acc = _dot(h_ref[...], w_ref[...])
    is_q = j % 3 == 0

    @pl.when(j == TILE_CQ)
    def _():
        qkv_ref[...] = (_rope_tile(acc, cos_ref[...], sin_ref[...]) * Q_SCALE_DIFF).astype(BF16)

    @pl.when(j == TILE_CK)
    def _():
        y = _rope_tile(acc, cos_ref[...], sin_ref[...])
        qkv_ref[...] = y.astype(BF16)
        _store_heads(st_refs[4], y)

    @pl.when(jnp.logical_and(is_q, j != TILE_CQ))
    def _():
        qkv_ref[...] = (acc * Q_SCALE).astype(BF16)

    @pl.when(jnp.logical_and(jnp.logical_not(is_q), j != TILE_CK))
    def _():
        qkv_ref[...] = acc.astype(BF16)

    for n, st_ref in enumerate(st_refs):
        if n == 4:
            continue
        cond = j == 3 * (n // 2) + 1 + n % 2
        if tail_rows and n in (2, 3):
            @pl.when(jnp.logical_and(cond, i % nper == nper - 1))
            def _(st_ref=st_ref):
                _store_heads(st_ref, acc[tm - tail_rows:, :])
        else:
            @pl.when(cond)
            def _(st_ref=st_ref):
                _store_heads(st_ref, acc)


def _proj(h, w_qkv, w_logit, fox_b, cos_t, sin_t, layer, depth, tm, tail_rows, bufs):
    m, d = h.shape
    tn = GROUP_W
    nt = m // tm
    nper = cos_t.shape[0] // tm
    full = pl.BlockSpec((tm * HEADS, LANES), lambda i, j: (layer * nt + i, 0))
    full_shape = jax.ShapeDtypeStruct((depth * m * HEADS, LANES), F32)
    if tail_rows:
        nstream = nt // nper
        bspec = pl.BlockSpec((tail_rows * HEADS, LANES), lambda i, j: (layer * nstream + i // nper, 0))
        bshape = jax.ShapeDtypeStruct((depth * nstream * tail_rows * HEADS, LANES), F32)
    else:
        bspec = pl.BlockSpec((tm * HEADS, LANES), lambda i, j: (i, 0))
        bshape = jax.ShapeDtypeStruct((m * HEADS, LANES), F32)
    st_specs = [full, full, bspec, bspec, full, full, full, full]
    st_shapes = [full_shape, full_shape, bshape, bshape, full_shape, full_shape, full_shape, full_shape]
    stacked = [n for n in range(8) if tail_rows or n not in (2, 3)]
    alias_in = [bufs[n] for n in stacked] if bufs is not None else []
    aliases = {6 + k: 2 + n for k, n in enumerate(stacked)} if bufs is not None else {}
    outs = pl.pallas_call(
        functools.partial(_proj_kernel, n_alias=len(alias_in), nper=nper, tail_rows=tail_rows),
        grid=(nt, QKV_W // tn),
        in_specs=[
            pl.BlockSpec((tm, d), lambda i, j: (i, 0)),
            pl.BlockSpec((None, d, tn), lambda i, j: (layer, 0, j)),
            pl.BlockSpec((None, d, LANES), lambda i, j: (layer, 0, 0)),
            pl.BlockSpec((None, 1, LANES), lambda i, j: (layer, 0, 0)),
            pl.BlockSpec((tm, LANES), lambda i, j: (i % nper, 0)),
            pl.BlockSpec((tm, LANES), lambda i, j: (i % nper, 0)),
        ] + [pl.BlockSpec(memory_space=pl.ANY)] * len(alias_in),
        out_specs=[
            pl.BlockSpec((tm, tn), lambda i, j: (i, j)),
            pl.BlockSpec((tm, LANES), lambda i, j: (i, 0)),
        ] + st_specs,
        out_shape=[
            jax.ShapeDtypeStruct((m, QKV_W), BF16),
            jax.ShapeDtypeStruct((m, LANES), F32),
        ] + st_shapes,
        input_output_aliases=aliases,
        compiler_params=_cparams("parallel", "arbitrary"),
        name="proj",
    )(h, w_qkv, w_logit, fox_b, cos_t, sin_t, *alias_in)
    return outs[0], outs[1], list(outs[2:])


def _norm_kernel(x_ref, g_ref, h_ref):
    h_ref[...] = _rmsnorm_rows(x_ref[...], g_ref[...]).astype(h_ref.dtype)


def _norm(x, g, layer, tm):
    m, d = x.shape
    return pl.pallas_call(
        _norm_kernel,
        grid=(m // tm,),
        in_specs=[pl.BlockSpec((tm, d), lambda i: (i, 0)), pl.BlockSpec((None, 1, d), lambda i: (layer, 0, 0))],
        out_specs=pl.BlockSpec((tm, d), lambda i: (i, 0)),
        out_shape=jax.ShapeDtypeStruct((m, d), BF16),
        compiler_params=_cparams("parallel"),
        name="norm",
    )(x, g)


def _out_proj_kernel(x_ref, oa_ref, ob_ref, oc_ref, od_ref, w_ref, g_ref, xo_ref, h_ref):
    acc = x_ref[...]
    for n, o_ref in enumerate((oa_ref, ob_ref, oc_ref, od_ref)):
        acc = acc + _dot(o_ref[...], w_ref[n * GROUP_W:(n + 1) * GROUP_W, :])
    xo_ref[...] = acc
    h_ref[...] = _rmsnorm_rows(acc, g_ref[...]).astype(BF16)


def _out_proj(x, o_list, w, g, layer, tm):
    m, d = x.shape
    row = lambda width: pl.BlockSpec((tm, width), lambda i: (i, 0))
    return pl.pallas_call(
        _out_proj_kernel,
        grid=(m // tm,),
        in_specs=[row(d)] + [row(GROUP_W)] * N_GROUPS + [
            pl.BlockSpec((None, N_GROUPS * GROUP_W, d), lambda i: (layer, 0, 0)),
            pl.BlockSpec((None, 1, d), lambda i: (layer, 0, 0)),
        ],
        out_specs=[row(d), row(d)],
        out_shape=[jax.ShapeDtypeStruct((m, d), F32), jax.ShapeDtypeStruct((m, d), BF16)],
        compiler_params=_cparams("parallel"),
        name="out_proj",
    )(x, *o_list, w, g)


def _up_kernel(h_ref, w_ref, u_ref):
    r = jnp.maximum(_dot(h_ref[...], w_ref[...]), 0.0)
    u_ref[...] = (r * r).astype(BF16)


def _up(h, w_up, layer, tm, tn):
    m, d = h.shape
    f = w_up.shape[2]
    return pl.pallas_call(
        _up_kernel,
        grid=(m // tm, f // tn),
        in_specs=[
            pl.BlockSpec((tm, d), lambda i, j: (i, 0)),
            pl.BlockSpec((None, d, tn), lambda i, j: (layer, 0, j)),
        ],
        out_specs=pl.BlockSpec((tm, tn), lambda i, j: (i, j)),
        out_shape=jax.ShapeDtypeStruct((m, f), BF16),
        compiler_params=_cparams("parallel", "arbitrary"),
        name="mlp_up",
    )(h, w_up)


def _down_kernel(x_ref, u_ref, w_ref, g_ref, xo_ref, h_ref):
    k = pl.program_id(1)
    part = _dot(u_ref[...], w_ref[...])

    @pl.when(k == 0)
    def _():
        xo_ref[...] = x_ref[...] + part

    @pl.when(k > 0)
    def _():
        xo_ref[...] += part

    @pl.when(k == pl.num_programs(1) - 1)
    def _():
        h_ref[...] = _rmsnorm_rows(xo_ref[...], g_ref[...]).astype(h_ref.dtype)


def _down(x, u, w, g, layer, h_dtype, tm, tk):
    m, d = x.shape
    f = u.shape[1]
    return pl.pallas_call(
        _down_kernel,
        grid=(m // tm, f // tk),
        in_specs=[
            pl.BlockSpec((tm, d), lambda i, k: (i, 0)),
            pl.BlockSpec((tm, tk), lambda i, k: (i, k)),
            pl.BlockSpec((None, tk, d), lambda i, k: (layer, k, 0)),
            pl.BlockSpec((None, 1, d), lambda i, k: (layer, 0, 0)),
        ],
        out_specs=[pl.BlockSpec((tm, d), lambda i, k: (i, 0)), pl.BlockSpec((tm, d), lambda i, k: (i, 0))],
        out_shape=[jax.ShapeDtypeStruct((m, d), F32), jax.ShapeDtypeStruct((m, d), h_dtype)],
        compiler_params=_cparams("parallel", "arbitrary"),
        name="mlp_down",
    )(x, u, w, g)


def _flash_init(rows):
    return (jnp.full((rows, 1), NEG_INF, F32), jnp.zeros((rows, 1), F32), jnp.zeros((rows, HEAD_DIM), F32))


def _flash_update(q, k, v, bias, mask, carry):
    m, l, acc = carry
    s = _dot_nt(q, k)
    if bias is not None:
        s = s + bias
    if mask is not None:
        s = jnp.where(mask, s, NEG_INF)
    m_new = jnp.maximum(m, jnp.max(s, axis=-1, keepdims=True))
    alpha = jnp.exp2(m - m_new)
    p = jnp.exp2(s - m_new)
    l = alpha * l + jnp.sum(p, axis=-1, keepdims=True)
    acc = alpha * acc + _dot(p.astype(BF16), v)
    return m_new, l, acc


def _stack_diff_q(q):
    lane = lax.broadcasted_iota(jnp.int32, q.shape, 1)
    zero = jnp.zeros_like(q)
    return jnp.concatenate([jnp.where(lane < DIFF_HALF, q, zero), jnp.where(lane >= DIFF_HALF, q, zero)], axis=0)


def _diff_finish(carry, tq, lam_ref, li_ref, g_ref):
    _, l, acc = carry
    o = acc / l
    lv = lam_ref[...]
    lam = (jnp.exp(jnp.sum(lv[0:1] * lv[1:2], axis=-1, keepdims=True))
           - jnp.exp(jnp.sum(lv[2:3] * lv[3:4], axis=-1, keepdims=True)) + li_ref[0])
    d = o[:tq] - lam * o[tq:]
    return _rmsnorm_rows(d, g_ref[...]) * (1.0 - li_ref[0])


def _sb_span(q, k, v, tri2, mask, run):
    nsub = k.shape[0] // LANES
    z = _dot_nt(q, k)
    neg_abs = pltpu.bitcast(pltpu.bitcast(z, jnp.uint32) | jnp.uint32(0x80000000), F32)
    sp = jnp.maximum(z, 0.0) + jnp.log2(1.0 + jnp.exp2(neg_abs))
    if mask is not None:
        sp = jnp.where(mask, sp, 0.0)
    hi = sp.astype(BF16)
    lo = (sp - hi.astype(F32)).astype(BF16)
    parts = [None] * nsub
    for n in reversed(range(nsub)):
        c = slice(n * LANES, (n + 1) * LANES)
        sums = _dot(jnp.concatenate([hi[:, c], lo[:, c]], axis=1), tri2)
        parts[n] = z[:, c] + sums[:, :LANES] + run
        run = run + sums[:, LANES:]
    a = jnp.exp2(jnp.concatenate(parts, axis=1))
    if mask is not None:
        a = jnp.where(mask, a, 0.0)
    return run, _dot(a.astype(BF16), v)


def _tri2():
    j = np.arange(2 * LANES)[:, None] % LANES
    s = np.arange(2 * LANES)[None, :]
    return jnp.asarray(np.where(np.where(s < LANES, j >= s, True), -1.0, 0.0), dtype=BF16)


def _key_norm_max(k_ref, kmax_ref):
    @pl.when(pl.program_id(2) == 0)
    def _():
        kmax_ref[...] = jnp.max(_row_norm(k_ref[...]), axis=0, keepdims=True)


def _fox_prompt_kernel(q_ref, k_ref, v_ref, b_ref, o_ref, kmax_ref, *, t):
    i = pl.program_id(2)
    _key_norm_max(k_ref, kmax_ref)
    q = q_ref[...]
    bound = _row_norm(q) * kmax_ref[...]

    def update(j, carry, mask):
        ks = pl.multiple_of(j * t, t)
        return _flash_update(q, k_ref[pl.ds(ks, t), :], v_ref[pl.ds(ks, t), :], b_ref[j], mask, carry)

    def visible(j, m):
        top = jnp.max(b_ref[jnp.maximum(j, 0)], axis=-1, keepdims=True)
        return jnp.max(bound + top - m) > EXP2_ZERO

    rows = lax.broadcasted_iota(jnp.int32, (t, t), 0)
    cols = lax.broadcasted_iota(jnp.int32, (t, t), 1)
    carry = update(i, _flash_init(t), cols <= rows)

    def body(c):
        carry = update(c[0], c[2:], None)
        return (c[0] - 1, visible(c[0] - 1, carry[0])) + carry

    c = lax.while_loop(lambda c: jnp.logical_and(c[0] >= 0, c[1]), body, (i - 1, visible(i - 1, carry[0])) + carry)
    o_ref[...] = (c[4] / c[3]).astype(BF16)


def _diff_prompt_kernel(q_ref, k_ref, v_ref, lam_ref, g_ref, li_ref, o_ref, *, tq, tk):
    i = pl.program_id(2)
    q2 = _stack_diff_q(q_ref[...])
    jd = (i * tq) // tk

    def step(j, carry, mask):
        ks = pl.multiple_of(j * tk, tk)
        return _flash_update(q2, k_ref[pl.ds(ks, tk), :], v_ref[pl.ds(ks, tk), :], None, mask, carry)

    carry = lax.fori_loop(0, jd, lambda j, c: step(j, c, None), _flash_init(2 * tq))
    rows = lax.broadcasted_iota(jnp.int32, (2 * tq, tk), 0) % tq + i * tq
    cols = lax.broadcasted_iota(jnp.int32, (2 * tq, tk), 1) + jd * tk
    carry = step(jd, carry, (cols // CHUNK) <= (rows // CHUNK))
    o_ref[...] = _diff_finish(carry, tq, lam_ref, li_ref, g_ref).astype(BF16)


def _sb_prompt_kernel(q_ref, k_ref, v_ref, tri_ref, o_ref, kmax_ref, run_ref, acc_ref, *, t):
    i = pl.program_id(2)
    _key_norm_max(k_ref, kmax_ref)
    q = q_ref[...]
    tri2 = tri_ref[...]
    bound = _row_norm(q) * kmax_ref[...]
    rows = lax.broadcasted_iota(jnp.int32, (t, t), 0)
    cols = lax.broadcasted_iota(jnp.int32, (t, t), 1)

    def span(j, mask, run):
        ks = pl.multiple_of(j * t, t)
        run, pv = _sb_span(q, k_ref[pl.ds(ks, t), :], v_ref[pl.ds(ks, t), :], tri2, mask, run)
        run_ref[...] = run
        return pv

    def visible():
        return jnp.max(bound + run_ref[:, :1]) > EXP2_ZERO

    acc_ref[...] = span(i, cols < rows, jnp.zeros((t, LANES), F32))

    def body(c):
        acc_ref[...] += span(c[0], None, run_ref[...])
        return c[0] - 1, visible()

    lax.while_loop(lambda c: jnp.logical_and(c[0] >= 0, c[1]), body, (i - 1, visible()))
    o_ref[...] = acc_ref[...].astype(BF16)


def _band_prompt_kernel(q_ref, kp_ref, kc_ref, vp_ref, vc_ref, bias_ref, o_ref, *, t):
    i = pl.program_id(2)
    q = q_ref[...]
    sp = _dot_nt(q, kp_ref[...]) + bias_ref[:, :t]
    sp = jnp.where(i > 0, sp, NEG_INF)
    sc = _dot_nt(q, kc_ref[...]) + bias_ref[:, t:]
    m = jnp.maximum(jnp.max(sp, axis=-1, keepdims=True), jnp.max(sc, axis=-1, keepdims=True))
    pp = jnp.exp2(sp - m)
    pc = jnp.exp2(sc - m)
    l = jnp.sum(pp, axis=-1, keepdims=True) + jnp.sum(pc, axis=-1, keepdims=True)
    o = _dot(pp.astype(BF16), vp_ref[...]) + _dot(pc.astype(BF16), vc_ref[...])
    o_ref[...] = (o / l).astype(BF16)


def _col(group, comp, h):
    return group * 3 * HEADS + comp * HEADS + h


def _prompt_call(kern, qkv, group, b, seq, tq, extra_in, extra_specs, scratch, name):
    nq = seq // tq
    m = qkv.shape[0]
    in_specs = [
        pl.BlockSpec((tq, LANES), lambda bi, h, i: (bi * nq + i, _col(group, 0, h))),
        pl.BlockSpec((seq, LANES), lambda bi, h, i: (bi, _col(group, 1, h))),
        pl.BlockSpec((seq, LANES), lambda bi, h, i: (bi, _col(group, 2, h))),
    ] + extra_specs
    return pl.pallas_call(
        kern,
        grid=(b, HEADS, nq),
        in_specs=in_specs,
        out_specs=pl.BlockSpec((tq, LANES), lambda bi, h, i: (bi * nq + i, h)),
        out_shape=jax.ShapeDtypeStruct((m, GROUP_W), BF16),
        scratch_shapes=scratch,
        compiler_params=_cparams("parallel", "parallel", "arbitrary"),
        name=name,
    )(qkv, qkv, qkv, *extra_in)


def _fox_prompt(qkv, key_bias, b, seq, t):
    nk = seq // t
    kb = jnp.swapaxes(key_bias, 1, 2).reshape(b * HEADS, nk, 1, t)
    spec = pl.BlockSpec((None, nk, 1, t), lambda bi, h, i: (bi * HEADS + h, 0, 0, 0))
    kern = functools.partial(_fox_prompt_kernel, t=t)
    return _prompt_call(kern, qkv, 0, b, seq, t, [kb], [spec], [pltpu.VMEM((1, 1), F32)], "fox_prompt")


def _diff_prompt(qkv, lam, dng, li, layer, b, seq, tq, tk):
    specs = [
        pl.BlockSpec((None, 4, DIFF_HALF), lambda bi, h, i: (layer, 0, 0)),
        pl.BlockSpec((None, 1, HEAD_DIM), lambda bi, h, i: (layer, 0, 0)),
        pl.BlockSpec(memory_space=pltpu.SMEM),
    ]
    kern = functools.partial(_diff_prompt_kernel, tq=tq, tk=tk)
    return _prompt_call(kern, qkv, 2, b, seq, tq, [lam, dng, li], specs, [], "diff_prompt")


def _sb_prompt(qkv, tri2, b, seq, t):
    spec = pl.BlockSpec((2 * LANES, 2 * LANES), lambda bi, h, i: (0, 0))
    kern = functools.partial(_sb_prompt_kernel, t=t)
    scratch = [pltpu.VMEM((1, 1), F32), pltpu.VMEM((t, LANES), F32), pltpu.VMEM((t, HEAD_DIM), F32)]
    return _prompt_call(kern, qkv, 3, b, seq, t, [tri2], [spec], scratch, "sb_prompt")


def _band_prompt(qkv, table, layer, b, seq, t):
    nq = seq // t
    m = qkv.shape[0]
    cur = lambda comp: (lambda bi, h, i: (bi * nq + i, _col(1, comp, h)))
    prev = lambda comp: (lambda bi, h, i: (bi * nq + jnp.maximum(i - 1, 0), _col(1, comp, h)))
    return pl.pallas_call(
        functools.partial(_band_prompt_kernel, t=t),
        grid=(b, HEADS, nq),
        in_specs=[
            pl.BlockSpec((t, LANES), cur(0)),
            pl.BlockSpec((t, LANES), prev(1)),
            pl.BlockSpec((t, LANES), cur(1)),
            pl.BlockSpec((t, LANES), prev(2)),
            pl.BlockSpec((t, LANES), cur(2)),
            pl.BlockSpec((None, None, t, 2 * t), lambda bi, h, i: (layer, h, 0, 0)),
        ],
        out_specs=pl.BlockSpec((t, LANES), lambda bi, h, i: (bi * nq + i, h)),
        out_shape=jax.ShapeDtypeStruct((m, GROUP_W), BF16),
        compiler_params=_cparams("parallel", "parallel", "arbitrary"),
        name="band_prompt",
    )(qkv, qkv, qkv, qkv, qkv, table)


def _cached_head(c_ref):
    rows = c_ref.shape[0] // HEADS
    return c_ref[pl.ds(pl.program_id(1), rows, stride=HEADS), :].astype(BF16)


def _fox_sample_kernel(q_ref, kn_ref, vn_ref, kc_ref, vc_ref, bc_ref, bn_ref, o_ref, *, ts):
    q = q_ref[...]
    carry = _flash_update(q, _cached_head(kc_ref), _cached_head(vc_ref), bc_ref[...], None, _flash_init(ts))
    rows = lax.broadcasted_iota(jnp.int32, (ts, ts), 0)
    cols = lax.broadcasted_iota(jnp.int32, (ts, ts), 1)
    _, l, acc = _flash_update(q, kn_ref[...], vn_ref[...], bn_ref[...], cols <= rows, carry)
    o_ref[...] = (acc / l).astype(BF16)


def _band_sample_kernel(q_ref, kn_ref, vn_ref, kc_ref, vc_ref, bias_ref, o_ref, *, ts, lb):
    q = q_ref[...]
    carry = _flash_update(q, _cached_head(kc_ref), _cached_head(vc_ref), bias_ref[:, :lb], None,
                          _flash_init(ts))
    _, l, acc = _flash_update(q, kn_ref[...], vn_ref[...], bias_ref[:, lb:], None, carry)
    o_ref[...] = (acc / l).astype(BF16)


def _diff_sample_kernel(q_ref, kn_ref, vn_ref, kc_ref, vc_ref, lam_ref, g_ref, li_ref, o_ref, *, ts):
    q2 = _stack_diff_q(q_ref[...])
    carry = _flash_update(q2, _cached_head(kc_ref), _cached_head(vc_ref), None, None, _flash_init(2 * ts))
    carry = _flash_update(q2, kn_ref[...], vn_ref[...], None, None, carry)
    o_ref[...] = _diff_finish(carry, ts, lam_ref, li_ref, g_ref).astype(BF16)


def _sb_sample_kernel(q_ref, kn_ref, vn_ref, kc_ref, vc_ref, tri_ref, o_ref, *, ts):
    q = q_ref[...]
    tri2 = tri_ref[...]
    pad = jnp.zeros((LANES - ts, HEAD_DIM), BF16)
    rows = lax.broadcasted_iota(jnp.int32, (ts, LANES), 0)
    cols = lax.broadcasted_iota(jnp.int32, (ts, LANES), 1)
    run, acc = _sb_span(q, jnp.concatenate([kn_ref[...], pad], axis=0), jnp.concatenate([vn_ref[...], pad], axis=0),
                        tri2, cols < rows, jnp.zeros((ts, LANES), F32))
    _, pv = _sb_span(q, _cached_head(kc_ref), _cached_head(vc_ref), tri2, None, run)
    o_ref[...] = (acc + pv).astype(BF16)


def _sample_call(kern, qkv, group, cache_k, cache_v, layer, bs, ts, extra_in, extra_specs, name):
    m = qkv.shape[0]
    rows = cache_k.shape[1]
    new = lambda comp: pl.BlockSpec((ts, LANES), lambda b, h: (b, _col(group, comp, h)))
    cache = pl.BlockSpec((None, rows, LANES), lambda b, h: (layer * bs + b, 0, 0))
    return pl.pallas_call(
        kern,
        grid=(bs, HEADS),
        in_specs=[new(0), new(1), new(2), cache, cache] + extra_specs,
        out_specs=pl.BlockSpec((ts, LANES), lambda b, h: (b, h)),
        out_shape=jax.ShapeDtypeStruct((m, GROUP_W), BF16),
        compiler_params=_cparams("parallel", "arbitrary"),
        name=name,
    )(qkv, qkv, qkv, cache_k, cache_v, *extra_in)


def _rope_tables(pos):
    inv = ROPE_THETA ** (-jnp.arange(ROPE_HALF, dtype=F32) / ROPE_HALF)
    ang = pos.astype(F32)[:, None] * inv[None, :]
    cos, sin = jnp.cos(ang), jnp.sin(ang)
    return jnp.tile(cos, (1, 4)), jnp.concatenate([-sin, sin, -sin, sin], axis=1)


def _band_table(rel, nrows, ncols, q_off):
    period = nrows + ncols
    off = np.arange(period)
    off = np.where(off < ncols, off, off - period)
    vec = rel.astype(F32)[:, :, np.clip(q_off - off, -REL_CLIP, REL_CLIP) + REL_CLIP] * LOG2E
    toep = jnp.tile(vec, (1, 1, nrows))[:, :, :nrows * (period - 1)]
    toep = toep.reshape(rel.shape[0], rel.shape[1], nrows, period - 1)[:, :, :, :ncols]
    qc = (q_off + np.arange(nrows)[:, None]) // CHUNK
    kc = np.arange(ncols)[None, :] // CHUNK
    vis = (kc <= qc) & (kc >= qc - BAND_CHUNKS)
    return jnp.where(jnp.asarray(vis), toep, NEG_INF)


def _row_tile(m, want):
    t = min(m, want)
    assert m % t == 0
    return t


def kernel(x_prompt, x_sample, cache_a_k, cache_a_v, cache_a_logf, cache_b_k, cache_b_v, cache_c_k, cache_c_v, cache_d_k, cache_d_v, attn_norm_g, w_in, fox_b, band_rel_bias, diff_lambda, diff_norm_g, w_out, mlp_norm_g, w_up, w_down, final_norm_g):
    b, seq, d = x_prompt.shape
    bs, ts, _ = x_sample.shape
    depth = w_in.shape[0]
    past = cache_a_k.shape[2]
    lb = cache_b_k.shape[2]
    lbp = min(BAND_PAST, seq)
    mp, ms = b * seq, bs * ts
    assert w_in.shape[2] == QKV_W + HEADS and d == N_GROUPS * GROUP_W
    assert ts == CHUNK and past % LANES == 0 and past % CHUNK == 0 and lb == BAND_PAST

    tm_p = _row_tile(seq, 1024)
    tm_s = _row_tile(ms, 1024)
    assert tm_s % ts == 0 and tm_p >= lbp
    tm_res_p, tm_res_s = _row_tile(mp, 512), _row_tile(ms, 512)
    tm_up_p, tm_up_s = _row_tile(mp, 1024), _row_tile(ms, 1024)
    tn_up = _row_tile(w_up.shape[2], 1024)
    tk_down = _row_tile(w_up.shape[2], 2048)
    t_attn = _row_tile(seq, 512)
    t_diff_k = _row_tile(seq, 1024)
    assert t_attn >= BAND_PAST and t_diff_k % t_attn == 0

    w_qkv = w_in[:, :, :QKV_W].astype(BF16)
    w_logit = jnp.pad(w_in[:, :, QKV_W:], ((0, 0), (0, 0), (0, LANES - HEADS))).astype(BF16)
    fox_b_pad = jnp.pad(fox_b.astype(F32), ((0, 0), (0, LANES - HEADS)))[:, None, :]
    w_out_b, w_up_b, w_down_b = w_out.astype(BF16), w_up.astype(BF16), w_down.astype(BF16)
    attn_g = attn_norm_g.astype(F32)[:, None, :]
    mlp_g = mlp_norm_g.astype(F32)[:, None, :]
    dng = diff_norm_g.astype(F32)[:, None, :]
    lam_rows = diff_lambda.astype(F32)
    cos_p, sin_p = _rope_tables(jnp.arange(seq))
    cos_s, sin_s = _rope_tables(past + jnp.arange(ts))
    cos_s, sin_s = jnp.tile(cos_s, (tm_s // ts, 1)), jnp.tile(sin_s, (tm_s // ts, 1))
    table_p = _band_table(band_rel_bias, t_attn, 2 * t_attn, t_attn)
    table_s = _band_table(band_rel_bias, ts, lb + ts, lb)
    tri2 = _tri2()

    flat = lambda c: c.reshape(depth * bs, c.shape[2] * HEADS, HEAD_DIM)
    ca_k, ca_v, cb_k, cb_v = flat(cache_a_k), flat(cache_a_v), flat(cache_b_k), flat(cache_b_v)
    cc_k, cc_v, cd_k, cd_v = flat(cache_c_k), flat(cache_c_v), flat(cache_d_k), flat(cache_d_v)

    next_g = jnp.concatenate([attn_g[1:], final_norm_g.astype(F32)[None, None, :]], axis=0)
    xp = x_prompt.reshape(mp, d)
    xs = x_sample.reshape(ms, d)
    hp = _norm(xp, attn_g, 0, tm_res_p)
    hs = _norm(xs, attn_g, 0, tm_res_s)
    p_st = s_st = None
    p_logf, s_logf, s_bk, s_bv = [], [], [], []
    for l in range(depth):
        lam_init = 0.8 - 0.6 * float(np.exp(-0.3 * l))
        li = jnp.asarray([lam_init], F32)

        h_dtype = BF16 if l + 1 < depth else F32
        qkv, logf, p_st = _proj(hp, w_qkv, w_logit, fox_b_pad, cos_p, sin_p, l, depth, tm_p, lbp, p_st)
        logf = logf[:, :HEADS].reshape(b, seq, HEADS)
        p_logf.append(logf)
        key_bias = jnp.cumsum(logf, axis=1) * (-LOG2E)
        oa = _fox_prompt(qkv, key_bias, b, seq, t_attn)
        ob = _band_prompt(qkv, table_p, l, b, seq, t_attn)
        oc = _diff_prompt(qkv, lam_rows, dng, li, l, b, seq, t_attn, t_diff_k)
        od = _sb_prompt(qkv, tri2, b, seq, t_attn)
        xp, hm = _out_proj(xp, [oa, ob, oc, od], w_out_b, mlp_g, l, tm_res_p)
        u = _up(hm, w_up_b, l, tm_up_p, tn_up)
        xp, hp = _down(xp, u, w_down_b, next_g, l, h_dtype, tm_res_p, tk_down)

        qkv, logf, s_st = _proj(hs, w_qkv, w_logit, fox_b_pad, cos_s, sin_s, l, depth, tm_s, 0, s_st)
        logf = logf[:, :HEADS].reshape(bs, ts, HEADS)
        s_logf.append(logf)
        s_bk.append(jnp.concatenate([cache_b_k[l][:, ts:], s_st[2].reshape(bs, ts, HEADS, HEAD_DIM)], axis=1))
        s_bv.append(jnp.concatenate([cache_b_v[l][:, ts:], s_st[3].reshape(bs, ts, HEADS, HEAD_DIM)], axis=1))
        cum_f = jnp.cumsum(jnp.concatenate([cache_a_logf[l].astype(F32), logf], axis=1), axis=1)
        kb = jnp.swapaxes(cum_f * (-LOG2E), 1, 2).reshape(bs * HEADS, 1, past + ts)
        kb_specs = [pl.BlockSpec((None, 1, past), lambda bi, h: (bi * HEADS + h, 0, 0)),
                    pl.BlockSpec((None, 1, ts), lambda bi, h: (bi * HEADS + h, 0, 0))]
        oa = _sample_call(functools.partial(_fox_sample_kernel, ts=ts), qkv, 0, ca_k, ca_v, l, bs, ts,
                          [kb[:, :, :past], kb[:, :, past:]], kb_specs, "fox_sample")
        ob = _sample_call(functools.partial(_band_sample_kernel, ts=ts, lb=lb), qkv, 1, cb_k, cb_v,
                          l, bs, ts, [table_s],
                          [pl.BlockSpec((None, None, ts, lb + ts), lambda bi, h: (l, h, 0, 0))], "band_sample")
        oc = _sample_call(functools.partial(_diff_sample_kernel, ts=ts), qkv, 2, cc_k, cc_v, l, bs, ts,
                          [lam_rows, dng, li],
                          [pl.BlockSpec((None, 4, DIFF_HALF), lambda bi, h: (l, 0, 0)),
                           pl.BlockSpec((None, 1, HEAD_DIM), lambda bi, h: (l, 0, 0)),
                           pl.BlockSpec(memory_space=pltpu.SMEM)], "diff_sample")
        od = _sample_call(functools.partial(_sb_sample_kernel, ts=ts), qkv, 3, cd_k, cd_v,
                          l, bs, ts, [tri2], [pl.BlockSpec((2 * LANES, 2 * LANES), lambda bi, h: (0, 0))],
                          "sb_sample")
        xs, hm = _out_proj(xs, [oa, ob, oc, od], w_out_b, mlp_g, l, tm_res_s)
        u = _up(hm, w_up_b, l, tm_up_s, tn_up)
        xs, hs = _down(xs, u, w_down_b, next_g, l, h_dtype, tm_res_s, tk_down)

    y_prompt = hp.reshape(b, seq, d)
    y_sample = hs.reshape(bs, ts, d)
    pst = lambda n, rows: p_st[n].reshape(depth, b, rows, HEADS, HEAD_DIM)
    sst = lambda n: s_st[n].reshape(depth, bs, ts, HEADS, HEAD_DIM)
    return (y_prompt, y_sample,
            pst(0, seq), pst(1, seq), jnp.stack(p_logf), pst(2, lbp), pst(3, lbp),
            pst(4, seq), pst(5, seq), pst(6, seq), pst(7, seq),
            sst(0), sst(1), jnp.stack(s_logf), jnp.stack(s_bk), jnp.stack(s_bv),
            sst(4), sst(5), sst(6), sst(7))
```

```python
import functools
import math

import numpy as np
import jax
import jax.numpy as jnp
from jax import lax
from jax.experimental import pallas as pl
from jax.experimental.pallas import tpu as pltpu

F32 = jnp.float32
BF16 = jnp.bfloat16

LANES = 128
HEAD_DIM = 128
HEADS = 4
GROUP_W = HEADS * HEAD_DIM
N_GROUPS = 4
QKV_W = 3 * N_GROUPS * GROUP_W
CHUNK = 64
BAND_CHUNKS = 8
BAND_PAST = BAND_CHUNKS * CHUNK
REL_CLIP = 128
DIFF_HALF = HEAD_DIM // 2
ROPE_HALF = DIFF_HALF // 2
ROPE_THETA = 10000.0
EPS = 1e-6
NEG_INF = -1e30
LOG2E = math.log2(math.e)
SCALE = HEAD_DIM ** -0.5
VMEM_LIMIT = 56 * 1024 * 1024

EXP2_ZERO = -160.0

TILE_CQ = 2 * 3 + 0
TILE_CK = 2 * 3 + 1
Q_SCALE = SCALE * LOG2E
Q_SCALE_DIFF = DIFF_HALF ** -0.5 * LOG2E


def _cparams(*sem):
    return pltpu.CompilerParams(dimension_semantics=sem, vmem_limit_bytes=VMEM_LIMIT)


def _dot(a, b):
    return jnp.dot(a, b, preferred_element_type=F32)


def _dot_nt(a, b):
    return lax.dot_general(a, b, (((1,), (1,)), ((), ())), preferred_element_type=F32)


def _rmsnorm_rows(x, g):
    return x * lax.rsqrt(jnp.mean(x * x, axis=-1, keepdims=True) + EPS) * g


def _log_sigmoid(z):
    return jnp.minimum(z, 0.0) - jnp.log1p(jnp.exp(-jnp.abs(z)))


def _row_norm(x):
    xf = x.astype(F32)
    return jnp.sqrt(jnp.sum(xf * xf, axis=-1, keepdims=True))


def _rope_tile(y, cos, sin):
    lane = lax.broadcasted_iota(jnp.int32, (1, LANES), 1)
    first = (lane % DIFF_HALF) < ROPE_HALF
    outs = []
    for h in range(HEADS):
        x = y[:, h * LANES:(h + 1) * LANES]
        partner = jnp.where(first, pltpu.roll(x, LANES - ROPE_HALF, 1), pltpu.roll(x, ROPE_HALF, 1))
        outs.append(x * cos + partner * sin)
    return jnp.concatenate(outs, axis=1)


def _store_heads(st_ref, val):
    rows = val.shape[0]
    for h in range(HEADS):
        st_ref[pl.ds(h, rows, stride=HEADS), :] = val[:, h * LANES:(h + 1) * LANES]


def _proj_kernel(*refs, n_alias, nper, tail_rows):
    h_ref, w_ref, wl_ref, fb_ref, cos_ref, sin_ref = refs[:6]
    qkv_ref, logf_ref = refs[6 + n_alias:8 + n_alias]
    st_refs = refs[8 + n_alias:16 + n_alias]
    i, j = pl.program_id(0), pl.program_id(1)
    tm = h_ref.shape[0]

    @pl.when(j == 0)
    def _():
        logf_ref[...] = _log_sigmoid(_dot(h_ref[...], wl_ref[...]) + fb_ref[...])

    acc = _dot(h_ref[...], w_ref[...])
    is_q = j % 3 == 0

    @pl.when(j == TILE_CQ)
    def _():
        qkv_ref[...] = (_rope_tile(acc, cos_ref[...], sin_ref[...]) * Q_SCALE_DIFF).astype(BF16)

    @pl.when(j == TILE_CK)
    def _():
        y = _rope_tile(acc, cos_ref[...], sin_ref[...])
        qkv_ref[...] = y.astype(BF16)
        _store_heads(st_refs[4], y)

    @pl.when(jnp.logical_and(is_q, j != TILE_CQ))
    def _():
        qkv_ref[...] = (acc * Q_SCALE).astype(BF16)

    @pl.when(jnp.logical_and(jnp.logical_not(is_q), j != TILE_CK))
    def _():
        qkv_ref[...] = acc.astype(BF16)

    for n, st_ref in enumerate(st_refs):
        if n == 4:
            continue
        cond = j == 3 * (n // 2) + 1 + n % 2
        if tail_rows and n in (2, 3):
            @pl.when(jnp.logical_and(cond, i % nper == nper - 1))
            def _(st_ref=st_ref):
                _store_heads(st_ref, acc[tm - tail_rows:, :])
        else:
            @pl.when(cond)
            def _(st_ref=st_ref):
                _store_heads(st_ref, acc)


def _proj(h, w_qkv, w_logit, fox_b, cos_t, sin_t, layer, depth, tm, tail_rows, bufs):
    m, d = h.shape
    tn = GROUP_W
    nt = m // tm
    nper = cos_t.shape[0] // tm
    full = pl.BlockSpec((tm * HEADS, LANES), lambda i, j: (layer * nt + i, 0))
    full_shape = jax.ShapeDtypeStruct((depth * m * HEADS, LANES), F32)
    if tail_rows:
        nstream = nt // nper
        bspec = pl.BlockSpec((tail_rows * HEADS, LANES), lambda i, j: (layer * nstream + i // nper, 0))
        bshape = jax.ShapeDtypeStruct((depth * nstream * tail_rows * HEADS, LANES), F32)
    else:
        bspec = pl.BlockSpec((tm * HEADS, LANES), lambda i, j: (i, 0))
        bshape = jax.ShapeDtypeStruct((m * HEADS, LANES), F32)
    st_specs = [full, full, bspec, bspec, full, full, full, full]
    st_shapes = [full_shape, full_shape, bshape, bshape, full_shape, full_shape, full_shape, full_shape]
    stacked = [n for n in range(8) if tail_rows or n not in (2, 3)]
    alias_in = [bufs[n] for n in stacked] if bufs is not None else []
    aliases = {6 + k: 2 + n for k, n in enumerate(stacked)} if bufs is not None else {}
    outs = pl.pallas_call(
        functools.partial(_proj_kernel, n_alias=len(alias_in), nper=nper, tail_rows=tail_rows),
        grid=(nt, QKV_W // tn),
        in_specs=[
            pl.BlockSpec((tm, d), lambda i, j: (i, 0)),
            pl.BlockSpec((None, d, tn), lambda i, j: (layer, 0, j)),
            pl.BlockSpec((None, d, LANES), lambda i, j: (layer, 0, 0)),
            pl.BlockSpec((None, 1, LANES), lambda i, j: (layer, 0, 0)),
            pl.BlockSpec((tm, LANES), lambda i, j: (i % nper, 0)),
            pl.BlockSpec((tm, LANES), lambda i, j: (i % nper, 0)),
        ] + [pl.BlockSpec(memory_space=pl.ANY)] * len(alias_in),
        out_specs=[
            pl.BlockSpec((tm, tn), lambda i, j: (i, j)),
            pl.BlockSpec((tm, LANES), lambda i, j: (i, 0)),
        ] + st_specs,
        out_shape=[
            jax.ShapeDtypeStruct((m, QKV_W), BF16),
            jax.ShapeDtypeStruct((m, LANES), F32),
        ] + st_shapes,
        input_output_aliases=aliases,
        compiler_params=_cparams("parallel", "arbitrary"),
        name="proj",
    )(h, w_qkv, w_logit, fox_b, cos_t, sin_t, *alias_in)
    return outs[0], outs[1], list(outs[2:])


def _norm_kernel(x_ref, g_ref, h_ref):
    h_ref[...] = _rmsnorm_rows(x_ref[...], g_ref[...]).astype(h_ref.dtype)


def _norm(x, g, layer, tm):
    m, d = x.shape
    return pl.pallas_call(
        _norm_kernel,
        grid=(m // tm,),
        in_specs=[pl.BlockSpec((tm, d), lambda i: (i, 0)), pl.BlockSpec((None, 1, d), lambda i: (layer, 0, 0))],
        out_specs=pl.BlockSpec((tm, d), lambda i: (i, 0)),
        out_shape=jax.ShapeDtypeStruct((m, d), BF16),
        compiler_params=_cparams("parallel"),
        name="norm",
    )(x, g)


def _out_proj_kernel(x_ref, oa_ref, ob_ref, oc_ref, od_ref, w_ref, g_ref, xo_ref, h_ref):
    acc = x_ref[...]
    for n, o_ref in enumerate((oa_ref, ob_ref, oc_ref, od_ref)):
        acc = acc + _dot(o_ref[...], w_ref[n * GROUP_W:(n + 1) * GROUP_W, :])
    xo_ref[...] = acc
    h_ref[...] = _rmsnorm_rows(acc, g_ref[...]).astype(BF16)


def _out_proj(x, o_list, w, g, layer, tm):
    m, d = x.shape
    row = lambda width: pl.BlockSpec((tm, width), lambda i: (i, 0))
    return pl.pallas_call(
        _out_proj_kernel,
        grid=(m // tm,),
        in_specs=[row(d)] + [row(GROUP_W)] * N_GROUPS + [
            pl.BlockSpec((None, N_GROUPS * GROUP_W, d), lambda i: (layer, 0, 0)),
            pl.BlockSpec((None, 1, d), lambda i: (layer, 0, 0)),
        ],
        out_specs=[row(d), row(d)],
        out_shape=[jax.ShapeDtypeStruct((m, d), F32), jax.ShapeDtypeStruct((m, d), BF16)],
        compiler_params=_cparams("parallel"),
        name="out_proj",
    )(x, *o_list, w, g)


def _up_kernel(h_ref, w_ref, u_ref):
    r = jnp.maximum(_dot(h_ref[...], w_ref[...]), 0.0)
    u_ref[...] = (r * r).astype(BF16)


def _up(h, w_up, layer, tm, tn):
    m, d = h.shape
    f = w_up.shape[2]
    return pl.pallas_call(
        _up_kernel,
        grid=(m // tm, f // tn),
        in_specs=[
            pl.BlockSpec((tm, d), lambda i, j: (i, 0)),
            pl.BlockSpec((None, d, tn), lambda i, j: (layer, 0, j)),
        ],
        out_specs=pl.BlockSpec((tm, tn), lambda i, j: (i, j)),
        out_shape=jax.ShapeDtypeStruct((m, f), BF16),
        compiler_params=_cparams("parallel", "arbitrary"),
        name="mlp_up",
    )(h, w_up)


def _down_kernel(x_ref, u_ref, w_ref, g_ref, xo_ref, h_ref):
    k = pl.program_id(1)

    @pl.when(k == 0)
    def _():
        xo_ref[...] = x_ref[...] + _dot(u_ref[...], w_ref[...])

    @pl.when(k > 0)
    def _():
        xo_ref[...] += _dot(u_ref[...], w_ref[...])

    @pl.when(k == pl.num_programs(1) - 1)
    def _():
        h_ref[...] = _rmsnorm_rows(xo_ref[...], g_ref[...]).astype(h_ref.dtype)


def _down(x, u, w, g, layer, h_dtype, tm, tk):
    m, d = x.shape
    f = u.shape[1]
    return pl.pallas_call(
        _down_kernel,
        grid=(m // tm, f // tk),
        in_specs=[
            pl.BlockSpec((tm, d), lambda i, k: (i, 0)),
            pl.BlockSpec((tm, tk), lambda i, k: (i, k)),
            pl.BlockSpec((None, tk, d), lambda i, k: (layer, k, 0)),
            pl.BlockSpec((None, 1, d), lambda i, k: (layer, 0, 0)),
        ],
        out_specs=[pl.BlockSpec((tm, d), lambda i, k: (i, 0)), pl.BlockSpec((tm, d), lambda i, k: (i, 0))],
        out_shape=[jax.ShapeDtypeStruct((m, d), F32), jax.ShapeDtypeStruct((m, d), h_dtype)],
        compiler_params=_cparams("parallel", "arbitrary"),
        name="mlp_down",
    )(x, u, w, g)


def _flash_init(rows):
    return (jnp.full((rows, 1), NEG_INF, F32), jnp.zeros((rows, 1), F32), jnp.zeros((rows, HEAD_DIM), F32))


def _flash_update(q, k, v, bias, mask, carry):
    m, l, acc = carry
    s = _dot_nt(q, k)
    if bias is not None:
        s = s + bias
    if mask is not None:
        s = jnp.where(mask, s, NEG_INF)
    m_new = jnp.maximum(m, jnp.max(s, axis=-1, keepdims=True))
    alpha = jnp.exp2(m - m_new)
    p = jnp.exp2(s - m_new)
    l = alpha * l + jnp.sum(p, axis=-1, keepdims=True)
    acc = alpha * acc + _dot(p.astype(BF16), v)
    return m_new, l, acc


def _stack_diff_q(q):
    lane = lax.broadcasted_iota(jnp.int32, q.shape, 1)
    zero = jnp.zeros_like(q)
    return jnp.concatenate([jnp.where(lane < DIFF_HALF, q, zero), jnp.where(lane >= DIFF_HALF, q, zero)], axis=0)


def _diff_finish(carry, tq, lam_ref, li_ref, g_ref):
    _, l, acc = carry
    o = acc / l
    lv = lam_ref[...]
    lam = (jnp.exp(jnp.sum(lv[0:1] * lv[1:2], axis=-1, keepdims=True))
           - jnp.exp(jnp.sum(lv[2:3] * lv[3:4], axis=-1, keepdims=True)) + li_ref[0])
    d = o[:tq] - lam * o[tq:]
    return _rmsnorm_rows(d, g_ref[...]) * (1.0 - li_ref[0])


def _sb_span(q, k, v, tri2, mask, run):
    nsub = k.shape[0] // LANES
    z = _dot_nt(q, k)
    neg_abs = pltpu.bitcast(pltpu.bitcast(z, jnp.uint32) | jnp.uint32(0x80000000), F32)
    sp = jnp.maximum(z, 0.0) + jnp.log2(1.0 + jnp.exp2(neg_abs))
    if mask is not None:
        sp = jnp.where(mask, sp, 0.0)
    hi = sp.astype(BF16)
    lo = (sp - hi.astype(F32)).astype(BF16)
    parts = [None] * nsub
    for n in reversed(range(nsub)):
        c = slice(n * LANES, (n + 1) * LANES)
        sums = _dot(jnp.concatenate([hi[:, c], lo[:, c]], axis=1), tri2)
        parts[n] = z[:, c] + sums[:, :LANES] + run
        run = run + sums[:, LANES:]
    a = jnp.exp2(jnp.concatenate(parts, axis=1))
    if mask is not None:
        a = jnp.where(mask, a, 0.0)
    return run, _dot(a.astype(BF16), v)


def _tri2():
    j = np.arange(2 * LANES)[:, None] % LANES
    s = np.arange(2 * LANES)[None, :]
    return jnp.asarray(np.where(np.where(s < LANES, j >= s, True), -1.0, 0.0), dtype=BF16)


def _key_norm_max(k_ref, kmax_ref):
    @pl.when(pl.program_id(2) == 0)
    def _():
        kmax_ref[...] = jnp.max(_row_norm(k_ref[...]), axis=0, keepdims=True)


def _fox_prompt_kernel(q_ref, k_ref, v_ref, b_ref, o_ref, kmax_ref, *, t):
    i = pl.program_id(2)
    _key_norm_max(k_ref, kmax_ref)
    q = q_ref[...]
    bound = _row_norm(q) * kmax_ref[...]

    def update(j, carry, mask):
        ks = pl.multiple_of(j * t, t)
        return _flash_update(q, k_ref[pl.ds(ks, t), :], v_ref[pl.ds(ks, t), :], b_ref[j], mask, carry)

    def visible(j, m):
        top = jnp.max(b_ref[jnp.maximum(j, 0)], axis=-1, keepdims=True)
        return jnp.max(bound + top - m) > EXP2_ZERO

    rows = lax.broadcasted_iota(jnp.int32, (t, t), 0)
    cols = lax.broadcasted_iota(jnp.int32, (t, t), 1)
    carry = update(i, _flash_init(t), cols <= rows)

    def body(c):
        carry = update(c[0], c[2:], None)
        return (c[0] - 1, visible(c[0] - 1, carry[0])) + carry

    c = lax.while_loop(lambda c: jnp.logical_and(c[0] >= 0, c[1]), body, (i - 1, visible(i - 1, carry[0])) + carry)
    o_ref[...] = (c[4] / c[3]).astype(BF16)


def _diff_prompt_kernel(q_ref, k_ref, v_ref, lam_ref, g_ref, li_ref, o_ref, *, tq, tk):
    i = pl.program_id(2)
    q2 = _stack_diff_q(q_ref[...])
    jd = (i * tq) // tk

    def step(j, carry, mask):
        ks = pl.multiple_of(j * tk, tk)
        return _flash_update(q2, k_ref[pl.ds(ks, tk), :], v_ref[pl.ds(ks, tk), :], None, mask, carry)

    carry = lax.fori_loop(0, jd // 2, lambda n, c: step(2 * n + 1, step(2 * n, c, None), None),
                          _flash_init(2 * tq))
    carry = lax.cond(jd % 2 == 1, lambda c: step(jd - 1, c, None), lambda c: c, carry)
    rows = lax.broadcasted_iota(jnp.int32, (2 * tq, tk), 0) % tq + i * tq
    cols = lax.broadcasted_iota(jnp.int32, (2 * tq, tk), 1) + jd * tk
    carry = step(jd, carry, (cols // CHUNK) <= (rows // CHUNK))
    o_ref[...] = _diff_finish(carry, tq, lam_ref, li_ref, g_ref).astype(BF16)


def _sb_prompt_kernel(q_ref, k_ref, v_ref, tri_ref, o_ref, kmax_ref, run_ref, acc_ref, *, t):
    i = pl.program_id(2)
    _key_norm_max(k_ref, kmax_ref)
    q = q_ref[...]
    tri2 = tri_ref[...]
    bound = _row_norm(q) * kmax_ref[...]
    rows = lax.broadcasted_iota(jnp.int32, (t, t), 0)
    cols = lax.broadcasted_iota(jnp.int32, (t, t), 1)

    def span(j, mask, run):
        ks = pl.multiple_of(j * t, t)
        run, pv = _sb_span(q, k_ref[pl.ds(ks, t), :], v_ref[pl.ds(ks, t), :], tri2, mask, run)
        run_ref[...] = run
        return pv

    def visible():
        return jnp.max(bound + run_ref[:, :1]) > EXP2_ZERO

    acc_ref[...] = span(i, cols < rows, jnp.zeros((t, LANES), F32))

    def body(c):
        acc_ref[...] += span(c[0], None, run_ref[...])
        return c[0] - 1, visible()

    lax.while_loop(lambda c: jnp.logical_and(c[0] >= 0, c[1]), body, (i - 1, visible()))
    o_ref[...] = acc_ref[...].astype(BF16)


def _band_prompt_kernel(q_ref, kp_ref, kc_ref, vp_ref, vc_ref, bias_ref, o_ref, *, t):
    i = pl.program_id(2)
    q = q_ref[...]
    sp = _dot_nt(q, kp_ref[...]) + bias_ref[:, :t]
    sp = jnp.where(i > 0, sp, NEG_INF)
    sc = _dot_nt(q, kc_ref[...]) + bias_ref[:, t:]
    m = jnp.maximum(jnp.max(sp, axis=-1, keepdims=True), jnp.max(sc, axis=-1, keepdims=True))
    pp = jnp.exp2(sp - m)
    pc = jnp.exp2(sc - m)
    l = jnp.sum(pp, axis=-1, keepdims=True) + jnp.sum(pc, axis=-1, keepdims=True)
    o = _dot(pp.astype(BF16), vp_ref[...]) + _dot(pc.astype(BF16), vc_ref[...])
    o_ref[...] = (o / l).astype(BF16)


def _col(group, comp, h):
    return group * 3 * HEADS + comp * HEADS + h


def _prompt_call(kern, qkv, group, b, seq, tq, extra_in, extra_specs, scratch, name):
    nq = seq // tq
    m = qkv.shape[0]
    in_specs = [
        pl.BlockSpec((tq, LANES), lambda bi, h, i: (bi * nq + i, _col(group, 0, h))),
        pl.BlockSpec((seq, LANES), lambda bi, h, i: (bi, _col(group, 1, h))),
        pl.BlockSpec((seq, LANES), lambda bi, h, i: (bi, _col(group, 2, h))),
    ] + extra_specs
    return pl.pallas_call(
        kern,
        grid=(b, HEADS, nq),
        in_specs=in_specs,
        out_specs=pl.BlockSpec((tq, LANES), lambda bi, h, i: (bi * nq + i, h)),
        out_shape=jax.ShapeDtypeStruct((m, GROUP_W), BF16),
        scratch_shapes=scratch,
        compiler_params=_cparams("parallel", "parallel", "arbitrary"),
        name=name,
    )(qkv, qkv, qkv, *extra_in)


def _fox_prompt(qkv, key_bias, b, seq, t):
    nk = seq // t
    kb = jnp.swapaxes(key_bias, 1, 2).reshape(b * HEADS, nk, 1, t)
    spec = pl.BlockSpec((None, nk, 1, t), lambda bi, h, i: (bi * HEADS + h, 0, 0, 0))
    kern = functools.partial(_fox_prompt_kernel, t=t)
    return _prompt_call(kern, qkv, 0, b, seq, t, [kb], [spec], [pltpu.VMEM((1, 1), F32)], "fox_prompt")


def _diff_prompt(qkv, lam, dng, li, layer, b, seq, tq, tk):
    specs = [
        pl.BlockSpec((None, 4, DIFF_HALF), lambda bi, h, i: (layer, 0, 0)),
        pl.BlockSpec((None, 1, HEAD_DIM), lambda bi, h, i: (layer, 0, 0)),
        pl.BlockSpec(memory_space=pltpu.SMEM),
    ]
    kern = functools.partial(_diff_prompt_kernel, tq=tq, tk=tk)
    return _prompt_call(kern, qkv, 2, b, seq, tq, [lam, dng, li], specs, [], "diff_prompt")


def _sb_prompt(qkv, tri2, b, seq, t):
    spec = pl.BlockSpec((2 * LANES, 2 * LANES), lambda bi, h, i: (0, 0))
    kern = functools.partial(_sb_prompt_kernel, t=t)
    scratch = [pltpu.VMEM((1, 1), F32), pltpu.VMEM((t, LANES), F32), pltpu.VMEM((t, HEAD_DIM), F32)]
    return _prompt_call(kern, qkv, 3, b, seq, t, [tri2], [spec], scratch, "sb_prompt")


def _band_prompt(qkv, table, layer, b, seq, t):
    nq = seq // t
    m = qkv.shape[0]
    cur = lambda comp: (lambda bi, h, i: (bi * nq + i, _col(1, comp, h)))
    prev = lambda comp: (lambda bi, h, i: (bi * nq + jnp.maximum(i - 1, 0), _col(1, comp, h)))
    return pl.pallas_call(
        functools.partial(_band_prompt_kernel, t=t),
        grid=(b, HEADS, nq),
        in_specs=[
            pl.BlockSpec((t, LANES), cur(0)),
            pl.BlockSpec((t, LANES), prev(1)),
            pl.BlockSpec((t, LANES), cur(1)),
            pl.BlockSpec((t, LANES), prev(2)),
            pl.BlockSpec((t, LANES), cur(2)),
            pl.BlockSpec((None, None, t, 2 * t), lambda bi, h, i: (layer, h, 0, 0)),
        ],
        out_specs=pl.BlockSpec((t, LANES), lambda bi, h, i: (bi * nq + i, h)),
        out_shape=jax.ShapeDtypeStruct((m, GROUP_W), BF16),
        compiler_params=_cparams("parallel", "parallel", "arbitrary"),
        name="band_prompt",
    )(qkv, qkv, qkv, qkv, qkv, table)


def _cached_head(c_ref, h):
    rows = c_ref.shape[0] // HEADS
    return c_ref[pl.ds(h, rows, stride=HEADS), :].astype(BF16)


def _head_cols(h):
    return slice(h * LANES, (h + 1) * LANES)


def _fox_sample_kernel(q_ref, kn_ref, vn_ref, kc_ref, vc_ref, bc_ref, bn_ref, o_ref, *, ts):
    rows = lax.broadcasted_iota(jnp.int32, (ts, ts), 0)
    cols = lax.broadcasted_iota(jnp.int32, (ts, ts), 1)
    for h in range(HEADS):
        c = _head_cols(h)
        q = q_ref[:, c]
        carry = _flash_update(q, _cached_head(kc_ref, h), _cached_head(vc_ref, h), bc_ref[h], None, _flash_init(ts))
        _, l, acc = _flash_update(q, kn_ref[:, c], vn_ref[:, c], bn_ref[h], cols <= rows, carry)
        o_ref[:, c] = (acc / l).astype(BF16)


def _band_sample_kernel(q_ref, kn_ref, vn_ref, kc_ref, vc_ref, bias_ref, o_ref, *, ts, lb):
    for h in range(HEADS):
        c = _head_cols(h)
        q = q_ref[:, c]
        carry = _flash_update(q, _cached_head(kc_ref, h), _cached_head(vc_ref, h), bias_ref[h, :, :lb], None,
                              _flash_init(ts))
        _, l, acc = _flash_update(q, kn_ref[:, c], vn_ref[:, c], bias_ref[h, :, lb:], None, carry)
        o_ref[:, c] = (acc / l).astype(BF16)


def _diff_sample_kernel(q_ref, kn_ref, vn_ref, kc_ref, vc_ref, lam_ref, g_ref, li_ref, o_ref, *, ts):
    for h in range(HEADS):
        c = _head_cols(h)
        q2 = _stack_diff_q(q_ref[:, c])
        carry = _flash_update(q2, _cached_head(kc_ref, h), _cached_head(vc_ref, h), None, None,
                              _flash_init(2 * ts))
        carry = _flash_update(q2, kn_ref[:, c], vn_ref[:, c], None, None, carry)
        o_ref[:, c] = _diff_finish(carry, ts, lam_ref, li_ref, g_ref).astype(BF16)


def _sb_sample_kernel(q_ref, kn_ref, vn_ref, kc_ref, vc_ref, tri_ref, o_ref, *, ts):
    tri2 = tri_ref[...]
    pad = jnp.zeros((LANES - ts, HEAD_DIM), BF16)
    rows = lax.broadcasted_iota(jnp.int32, (ts, LANES), 0)
    cols = lax.broadcasted_iota(jnp.int32, (ts, LANES), 1)
    for h in range(HEADS):
        c = _head_cols(h)
        q = q_ref[:, c]
        run, acc = _sb_span(q, jnp.concatenate([kn_ref[:, c], pad], axis=0),
                            jnp.concatenate([vn_ref[:, c], pad], axis=0), tri2, cols < rows,
                            jnp.zeros((ts, LANES), F32))
        _, pv = _sb_span(q, _cached_head(kc_ref, h), _cached_head(vc_ref, h), tri2, None, run)
        o_ref[:, c] = (acc + pv).astype(BF16)


def _sample_call(kern, qkv, group, cache_k, cache_v, layer, bs, ts, extra_in, extra_specs, name):
    m = qkv.shape[0]
    rows = cache_k.shape[1]
    new = lambda comp: pl.BlockSpec((ts, GROUP_W), lambda b: (b, group * 3 + comp))
    cache = pl.BlockSpec((None, rows, LANES), lambda b: (layer * bs + b, 0, 0))
    return pl.pallas_call(
        kern,
        grid=(bs,),
        in_specs=[new(0), new(1), new(2), cache, cache] + extra_specs,
        out_specs=pl.BlockSpec((ts, GROUP_W), lambda b: (b, 0)),
        out_shape=jax.ShapeDtypeStruct((m, GROUP_W), BF16),
        compiler_params=_cparams("parallel"),
        name=name,
    )(qkv, qkv, qkv, cache_k, cache_v, *extra_in)


def _rope_tables(pos):
    inv = ROPE_THETA ** (-jnp.arange(ROPE_HALF, dtype=F32) / ROPE_HALF)
    ang = pos.astype(F32)[:, None] * inv[None, :]
    cos, sin = jnp.cos(ang), jnp.sin(ang)
    return jnp.tile(cos, (1, 4)), jnp.concatenate([-sin, sin, -sin, sin], axis=1)


def _band_table(rel, nrows, ncols, q_off):
    period = nrows + ncols
    off = np.arange(period)
    off = np.where(off < ncols, off, off - period)
    vec = rel.astype(F32)[:, :, np.clip(q_off - off, -REL_CLIP, REL_CLIP) + REL_CLIP] * LOG2E
    toep = jnp.tile(vec, (1, 1, nrows))[:, :, :nrows * (period - 1)]
    toep = toep.reshape(rel.shape[0], rel.shape[1], nrows, period - 1)[:, :, :, :ncols]
    qc = (q_off + np.arange(nrows)[:, None]) // CHUNK
    kc = np.arange(ncols)[None, :] // CHUNK
    vis = (kc <= qc) & (kc >= qc - BAND_CHUNKS)
    return jnp.where(jnp.asarray(vis), toep, NEG_INF)


def _row_tile(m, want):
    t = min(m, want)
    assert m % t == 0
    return t


def kernel(x_prompt, x_sample, cache_a_k, cache_a_v, cache_a_logf, cache_b_k, cache_b_v, cache_c_k, cache_c_v, cache_d_k, cache_d_v, attn_norm_g, w_in, fox_b, band_rel_bias, diff_lambda, diff_norm_g, w_out, mlp_norm_g, w_up, w_down, final_norm_g):
    b, seq, d = x_prompt.shape
    bs, ts, _ = x_sample.shape
    depth = w_in.shape[0]
    past = cache_a_k.shape[2]
    lb = cache_b_k.shape[2]
    lbp = min(BAND_PAST, seq)
    mp, ms = b * seq, bs * ts
    assert w_in.shape[2] == QKV_W + HEADS and d == N_GROUPS * GROUP_W
    assert ts == CHUNK and past % LANES == 0 and past % CHUNK == 0 and lb == BAND_PAST

    tm_p = _row_tile(seq, 1024)
    tm_s = _row_tile(ms, 1024)
    assert tm_s % ts == 0 and tm_p >= lbp
    tm_res_p, tm_res_s = _row_tile(mp, 512), _row_tile(ms, 512)
    tm_up_p, tm_up_s = _row_tile(mp, 1024), _row_tile(ms, 1024)
    tn_up = _row_tile(w_up.shape[2], 1024)
    tk_down = _row_tile(w_up.shape[2], 2048)
    t_attn = _row_tile(seq, 512)
    t_diff_k = _row_tile(seq, 1024)
    assert t_attn >= BAND_PAST and t_diff_k % t_attn == 0

    w_qkv = w_in[:, :, :QKV_W].astype(BF16)
    w_logit = jnp.pad(w_in[:, :, QKV_W:], ((0, 0), (0, 0), (0, LANES - HEADS))).astype(BF16)
    fox_b_pad = jnp.pad(fox_b.astype(F32), ((0, 0), (0, LANES - HEADS)))[:, None, :]
    w_out_b, w_up_b, w_down_b = w_out.astype(BF16), w_up.astype(BF16), w_down.astype(BF16)
    attn_g = attn_norm_g.astype(F32)[:, None, :]
    mlp_g = mlp_norm_g.astype(F32)[:, None, :]
    dng = diff_norm_g.astype(F32)[:, None, :]
    lam_rows = diff_lambda.astype(F32)
    cos_p, sin_p = _rope_tables(jnp.arange(seq))
    cos_s, sin_s = _rope_tables(past + jnp.arange(ts))
    cos_s, sin_s = jnp.tile(cos_s, (tm_s // ts, 1)), jnp.tile(sin_s, (tm_s // ts, 1))
    table_p = _band_table(band_rel_bias, t_attn, 2 * t_attn, t_attn)
    table_s = _band_table(band_rel_bias, ts, lb + ts, lb)
    tri2 = _tri2()

    flat = lambda c: c.reshape(depth * bs, c.shape[2] * HEADS, HEAD_DIM)
    ca_k, ca_v, cb_k, cb_v = flat(cache_a_k), flat(cache_a_v), flat(cache_b_k), flat(cache_b_v)
    cc_k, cc_v, cd_k, cd_v = flat(cache_c_k), flat(cache_c_v), flat(cache_d_k), flat(cache_d_v)

    next_g = jnp.concatenate([attn_g[1:], final_norm_g.astype(F32)[None, None, :]], axis=0)
    xp = x_prompt.reshape(mp, d)
    xs = x_sample.reshape(ms, d)
    hp = _norm(xp, attn_g, 0, tm_res_p)
    hs = _norm(xs, attn_g, 0, tm_res_s)
    p_st = s_st = None
    p_logf, s_logf, s_bk, s_bv = [], [], [], []
    for l in range(depth):
        lam_init = 0.8 - 0.6 * float(np.exp(-0.3 * l))
        li = jnp.asarray([lam_init], F32)

        h_dtype = BF16 if l + 1 < depth else F32
        qkv, logf, p_st = _proj(hp, w_qkv, w_logit, fox_b_pad, cos_p, sin_p, l, depth, tm_p, lbp, p_st)
        logf = logf[:, :HEADS].reshape(b, seq, HEADS)
        p_logf.append(logf)
        key_bias = jnp.cumsum(logf, axis=1) * (-LOG2E)
        oa = _fox_prompt(qkv, key_bias, b, seq, t_attn)
        ob = _band_prompt(qkv, table_p, l, b, seq, t_attn)
        oc = _diff_prompt(qkv, lam_rows, dng, li, l, b, seq, t_attn, t_diff_k)
        od = _sb_prompt(qkv, tri2, b, seq, t_attn)
        xp, hm = _out_proj(xp, [oa, ob, oc, od], w_out_b, mlp_g, l, tm_res_p)
        u = _up(hm, w_up_b, l, tm_up_p, tn_up)
        xp, hp = _down(xp, u, w_down_b, next_g, l, h_dtype, tm_res_p, tk_down)

        qkv, logf, s_st = _proj(hs, w_qkv, w_logit, fox_b_pad, cos_s, sin_s, l, depth, tm_s, 0, s_st)
        logf = logf[:, :HEADS].reshape(bs, ts, HEADS)
        s_logf.append(logf)
        s_bk.append(jnp.concatenate([cache_b_k[l][:, ts:], s_st[2].reshape(bs, ts, HEADS, HEAD_DIM)], axis=1))
        s_bv.append(jnp.concatenate([cache_b_v[l][:, ts:], s_st[3].reshape(bs, ts, HEADS, HEAD_DIM)], axis=1))
        cum_f = jnp.cumsum(jnp.concatenate([cache_a_logf[l].astype(F32), logf], axis=1), axis=1)
        kb = jnp.swapaxes(cum_f * (-LOG2E), 1, 2).reshape(bs, HEADS, 1, past + ts)
        kb_specs = [pl.BlockSpec((None, HEADS, 1, past), lambda bi: (bi, 0, 0, 0)),
                    pl.BlockSpec((None, HEADS, 1, ts), lambda bi: (bi, 0, 0, 0))]
        oa = _sample_call(functools.partial(_fox_sample_kernel, ts=ts), qkv, 0, ca_k, ca_v, l, bs, ts,
                          [kb[..., :past], kb[..., past:]], kb_specs, "fox_sample")
        ob = _sample_call(functools.partial(_band_sample_kernel, ts=ts, lb=lb), qkv, 1, cb_k, cb_v,
                          l, bs, ts, [table_s],
                          [pl.BlockSpec((None, HEADS, ts, lb + ts), lambda bi: (l, 0, 0, 0))], "band_sample")
        oc = _sample_call(functools.partial(_diff_sample_kernel, ts=ts), qkv, 2, cc_k, cc_v, l, bs, ts,
                          [lam_rows, dng, li],
                          [pl.BlockSpec((None, 4, DIFF_HALF), lambda bi: (l, 0, 0)),
                           pl.BlockSpec((None, 1, HEAD_DIM), lambda bi: (l, 0, 0)),
                           pl.BlockSpec(memory_space=pltpu.SMEM)], "diff_sample")
        od = _sample_call(functools.partial(_sb_sample_kernel, ts=ts), qkv, 3, cd_k, cd_v,
                          l, bs, ts, [tri2], [pl.BlockSpec((2 * LANES, 2 * LANES), lambda bi: (0, 0))],
                          "sb_sample")
        xs, hm = _out_proj(xs, [oa, ob, oc, od], w_out_b, mlp_g, l, tm_res_s)
        u = _up(hm, w_up_b, l, tm_up_s, tn_up)
        xs, hs = _down(xs, u, w_down_b, next_g, l, h_dtype, tm_res_s, tk_down)

    y_prompt = hp.reshape(b, seq, d)
    y_sample = hs.reshape(bs, ts, d)
    pst = lambda n, rows: p_st[n].reshape(depth, b, rows, HEADS, HEAD_DIM)
    sst = lambda n: s_st[n].reshape(depth, bs, ts, HEADS, HEAD_DIM)
    return (y_prompt, y_sample,
            pst(0, seq), pst(1, seq), jnp.stack(p_logf), pst(2, lbp), pst(3, lbp),
            pst(4, seq), pst(5, seq), pst(6, seq), pst(7, seq),
            sst(0), sst(1), jnp.stack(s_logf), jnp.stack(s_bk), jnp.stack(s_bv),
            sst(4), sst(5), sst(6), sst(7))
```

```python
import functools
import math

import numpy as np
import jax
import jax.numpy as jnp
from jax import lax
from jax.experimental import pallas as pl
from jax.experimental.pallas import tpu as pltpu

F32 = jnp.float32
BF16 = jnp.bfloat16

LANES = 128
HEAD_DIM = 128
HEADS = 4
GROUP_W = HEADS * HEAD_DIM
N_GROUPS = 4
QKV_W = 3 * N_GROUPS * GROUP_W
CHUNK = 64
BAND_CHUNKS = 8
BAND_PAST = BAND_CHUNKS * CHUNK
REL_CLIP = 128
DIFF_HALF = HEAD_DIM // 2
ROPE_HALF = DIFF_HALF // 2
ROPE_THETA = 10000.0
EPS = 1e-6
NEG_INF = -1e30
LOG2E = math.log2(math.e)
SCALE = HEAD_DIM ** -0.5
VMEM_LIMIT = 56 * 1024 * 1024

EXP2_ZERO = -160.0

TILE_CQ = 2 * 3 + 0
TILE_CK = 2 * 3 + 1
Q_SCALE = SCALE * LOG2E
Q_SCALE_DIFF = DIFF_HALF ** -0.5 * LOG2E


def _cparams(*sem):
    return pltpu.CompilerParams(dimension_semantics=sem, vmem_limit_bytes=VMEM_LIMIT)


def _dot(a, b):
    return jnp.dot(a, b, preferred_element_type=F32)


def _dot_nt(a, b):
    return lax.dot_general(a, b, (((1,), (1,)), ((), ())), preferred_element_type=F32)


def _rmsnorm_rows(x, g):
    return x * lax.rsqrt(jnp.mean(x * x, axis=-1, keepdims=True) + EPS) * g


def _log_sigmoid(z):
    return jnp.minimum(z, 0.0) - jnp.log1p(jnp.exp(-jnp.abs(z)))


def _row_norm(x):
    xf = x.astype(F32)
    return jnp.sqrt(jnp.sum(xf * xf, axis=-1, keepdims=True))


def _rope_tile(y, cos, sin):
    lane = lax.broadcasted_iota(jnp.int32, (1, LANES), 1)
    first = (lane % DIFF_HALF) < ROPE_HALF
    outs = []
    for h in range(HEADS):
        x = y[:, h * LANES:(h + 1) * LANES]
        partner = jnp.where(first, pltpu.roll(x, LANES - ROPE_HALF, 1), pltpu.roll(x, ROPE_HALF, 1))
        outs.append(x * cos + partner * sin)
    return jnp.concatenate(outs, axis=1)


def _store_heads(st_ref, val):
    rows = val.shape[0]
    for h in range(HEADS):
        st_ref[pl.ds(h, rows, stride=HEADS), :] = val[:, h * LANES:(h + 1) * LANES]


def _proj_kernel(*refs, n_alias, nper, tail_rows):
    h_ref, w_ref, wl_ref, fb_ref, cos_ref, sin_ref = refs[:6]
    qkv_ref, logf_ref = refs[6 + n_alias:8 + n_alias]
    st_refs = refs[8 + n_alias:16 + n_alias]
    i, j = pl.program_id(0), pl.program_id(1)
    tm = h_ref.shape[0]

    @pl.when(j == 0)
    def _():
        logf_ref[...] = _log_sigmoid(_dot(h_ref[...], wl_ref[...]) + fb_ref[...])

    acc = _dot(h_ref[...], w_ref[...])
    is_q = j % 3 == 0

    @pl.when(j == TILE_CQ)
    def _():
        qkv_ref[...] = (_rope_tile(acc, cos_ref[...], sin_ref[...]) * Q_SCALE_DIFF).astype(BF16)

    @pl.when(j == TILE_CK)
    def _():
        y = _rope_tile(acc, cos_ref[...], sin_ref[...])
        qkv_ref[...] = y.astype(BF16)
        _store_heads(st_refs[4], y)

    @pl.when(jnp.logical_and(is_q, j != TILE_CQ))
    def _():
        qkv_ref[...] = (acc * Q_SCALE).astype(BF16)

    @pl.when(jnp.logical_and(jnp.logical_not(is_q), j != TILE_CK))
    def _():
        qkv_ref[...] = acc.astype(BF16)

    for n, st_ref in enumerate(st_refs):
        if n == 4:
            continue
        cond = j == 3 * (n // 2) + 1 + n % 2
        if tail_rows and n in (2, 3):
            @pl.when(jnp.logical_and(cond, i % nper == nper - 1))
            def _(st_ref=st_ref):
                _store_heads(st_ref, acc[tm - tail_rows:, :])
        else:
            @pl.when(cond)
            def _(st_ref=st_ref):
                _store_heads(st_ref, acc)


def _proj(h, w_qkv, w_logit, fox_b, cos_t, sin_t, layer, depth, tm, tail_rows, bufs):
    m, d = h.shape
    tn = GROUP_W
    nt = m // tm
    nper = cos_t.shape[0] // tm
    full = pl.BlockSpec((tm * HEADS, LANES), lambda i, j: (layer * nt + i, 0))
    full_shape = jax.ShapeDtypeStruct((depth * m * HEADS, LANES), F32)
    if tail_rows:
        nstream = nt // nper
        bspec = pl.BlockSpec((tail_rows * HEADS, LANES), lambda i, j: (layer * nstream + i // nper, 0))
        bshape = jax.ShapeDtypeStruct((depth * nstream * tail_rows * HEADS, LANES), F32)
    else:
        bspec = pl.BlockSpec((tm * HEADS, LANES), lambda i, j: (i, 0))
        bshape = jax.ShapeDtypeStruct((m * HEADS, LANES), F32)
    st_specs = [full, full, bspec, bspec, full, full, full, full]
    st_shapes = [full_shape, full_shape, bshape, bshape, full_shape, full_shape, full_shape, full_shape]
    stacked = [n for n in range(8) if tail_rows or n not in (2, 3)]
    alias_in = [bufs[n] for n in stacked] if bufs is not None else []
    aliases = {6 + k: 2 + n for k, n in enumerate(stacked)} if bufs is not None else {}
    outs = pl.pallas_call(
        functools.partial(_proj_kernel, n_alias=len(alias_in), nper=nper, tail_rows=tail_rows),
        grid=(nt, QKV_W // tn),
        in_specs=[
            pl.BlockSpec((tm, d), lambda i, j: (i, 0)),
            pl.BlockSpec((None, d, tn), lambda i, j: (layer, 0, j)),
            pl.BlockSpec((None, d, LANES), lambda i, j: (layer, 0, 0)),
            pl.BlockSpec((None, 1, LANES), lambda i, j: (layer, 0, 0)),
            pl.BlockSpec((tm, LANES), lambda i, j: (i % nper, 0)),
            pl.BlockSpec((tm, LANES), lambda i, j: (i % nper, 0)),
        ] + [pl.BlockSpec(memory_space=pl.ANY)] * len(alias_in),
        out_specs=[
            pl.BlockSpec((tm, tn), lambda i, j: (i, j)),
            pl.BlockSpec((tm, LANES), lambda i, j: (i, 0)),
        ] + st_specs,
        out_shape=[
            jax.ShapeDtypeStruct((m, QKV_W), BF16),
            jax.ShapeDtypeStruct((m, LANES), F32),
        ] + st_shapes,
        input_output_aliases=aliases,
        compiler_params=_cparams("parallel", "arbitrary"),
        name="proj",
    )(h, w_qkv, w_logit, fox_b, cos_t, sin_t, *alias_in)
    return outs[0], outs[1], list(outs[2:])


def _norm_kernel(x_ref, g_ref, h_ref):
    h_ref[...] = _rmsnorm_rows(x_ref[...], g_ref[...]).astype(h_ref.dtype)


def _norm(x, g, layer, tm):
    m, d = x.shape
    return pl.pallas_call(
        _norm_kernel,
        grid=(m // tm,),
        in_specs=[pl.BlockSpec((tm, d), lambda i: (i, 0)), pl.BlockSpec((None, 1, d), lambda i: (layer, 0, 0))],
        out_specs=pl.BlockSpec((tm, d), lambda i: (i, 0)),
        out_shape=jax.ShapeDtypeStruct((m, d), BF16),
        compiler_params=_cparams("parallel"),
        name="norm",
    )(x, g)


def _out_proj_kernel(x_ref, oa_ref, ob_ref, oc_ref, od_ref, w_ref, g_ref, xo_ref, h_ref):
    acc = x_ref[...]
    for n, o_ref in enumerate((oa_ref, ob_ref, oc_ref, od_ref)):
        acc = acc + _dot(o_ref[...], w_ref[n * GROUP_W:(n + 1) * GROUP_W, :])
    xo_ref[...] = acc
    h_ref[...] = _rmsnorm_rows(acc, g_ref[...]).astype(BF16)


def _out_proj(x, o_list, w, g, layer, tm):
    m, d = x.shape
    row = lambda width: pl.BlockSpec((tm, width), lambda i: (i, 0))
    return pl.pallas_call(
        _out_proj_kernel,
        grid=(m // tm,),
        in_specs=[row(d)] + [row(GROUP_W)] * N_GROUPS + [
            pl.BlockSpec((None, N_GROUPS * GROUP_W, d), lambda i: (layer, 0, 0)),
            pl.BlockSpec((None, 1, d), lambda i: (layer, 0, 0)),
        ],
        out_specs=[row(d), row(d)],
        out_shape=[jax.ShapeDtypeStruct((m, d), F32), jax.ShapeDtypeStruct((m, d), BF16)],
        compiler_params=_cparams("parallel"),
        name="out_proj",
    )(x, *o_list, w, g)


def _up_kernel(h_ref, w_ref, u_ref):
    r = jnp.maximum(_dot(h_ref[...], w_ref[...]), 0.0)
    u_ref[...] = (r * r).astype(BF16)


def _up(h, w_up, layer, tm, tn):
    m, d = h.shape
    f = w_up.shape[2]
    return pl.pallas_call(
        _up_kernel,
        grid=(m // tm, f // tn),
        in_specs=[
            pl.BlockSpec((tm, d), lambda i, j: (i, 0)),
            pl.BlockSpec((None, d, tn), lambda i, j: (layer, 0, j)),
        ],
        out_specs=pl.BlockSpec((tm, tn), lambda i, j: (i, j)),
        out_shape=jax.ShapeDtypeStruct((m, f), BF16),
        compiler_params=_cparams("parallel", "arbitrary"),
        name="mlp_up",
    )(h, w_up)


def _down_kernel(x_ref, u_ref, w_ref, g_ref, xo_ref, h_ref):
    k = pl.program_id(1)

    @pl.when(k == 0)
    def _():
        xo_ref[...] = x_ref[...] + _dot(u_ref[...], w_ref[...])

    @pl.when(k > 0)
    def _():
        xo_ref[...] += _dot(u_ref[...], w_ref[...])

    @pl.when(k == pl.num_programs(1) - 1)
    def _():
        h_ref[...] = _rmsnorm_rows(xo_ref[...], g_ref[...]).astype(h_ref.dtype)


def _down(x, u, w, g, layer, h_dtype, tm, tk):
    m, d = x.shape
    f = u.shape[1]
    return pl.pallas_call(
        _down_kernel,
        grid=(m // tm, f // tk),
        in_specs=[
            pl.BlockSpec((tm, d), lambda i, k: (i, 0)),
            pl.BlockSpec((tm, tk), lambda i, k: (i, k)),
            pl.BlockSpec((None, tk, d), lambda i, k: (layer, k, 0)),
            pl.BlockSpec((None, 1, d), lambda i, k: (layer, 0, 0)),
        ],
        out_specs=[pl.BlockSpec((tm, d), lambda i, k: (i, 0)), pl.BlockSpec((tm, d), lambda i, k: (i, 0))],
        out_shape=[jax.ShapeDtypeStruct((m, d), F32), jax.ShapeDtypeStruct((m, d), h_dtype)],
        compiler_params=_cparams("parallel", "arbitrary"),
        name="mlp_down",
    )(x, u, w, g)


def _flash_init(rows):
    return (jnp.full((rows, 1), NEG_INF, F32), jnp.zeros((rows, 1), F32), jnp.zeros((rows, HEAD_DIM), F32))


def _flash_update(q, k, v, bias, mask, carry):
    m, l, acc = carry
    s = _dot_nt(q, k)
    if bias is not None:
        s = s + bias
    if mask is not None:
        s = jnp.where(mask, s, NEG_INF)
    m_new = jnp.maximum(m, jnp.max(s, axis=-1, keepdims=True))
    alpha = jnp.exp2(m - m_new)
    p = jnp.exp2(s - m_new)
    l = alpha * l + jnp.sum(p, axis=-1, keepdims=True)
    acc = alpha * acc + _dot(p.astype(BF16), v)
    return m_new, l, acc


def _stack_diff_q(q):
    lane = lax.broadcasted_iota(jnp.int32, q.shape, 1)
    zero = jnp.zeros_like(q)
    return jnp.concatenate([jnp.where(lane < DIFF_HALF, q, zero), jnp.where(lane >= DIFF_HALF, q, zero)], axis=0)


def _diff_finish(carry, tq, lam_ref, li_ref, g_ref):
    _, l, acc = carry
    o = acc / l
    lv = lam_ref[...]
    lam = (jnp.exp(jnp.sum(lv[0:1] * lv[1:2], axis=-1, keepdims=True))
           - jnp.exp(jnp.sum(lv[2:3] * lv[3:4], axis=-1, keepdims=True)) + li_ref[0])
    d = o[:tq] - lam * o[tq:]
    return _rmsnorm_rows(d, g_ref[...]) * (1.0 - li_ref[0])


def _sb_span(q, k, v, tri2, mask, run):
    nsub = k.shape[0] // LANES
    z = _dot_nt(q, k)
    neg_abs = pltpu.bitcast(pltpu.bitcast(z, jnp.uint32) | jnp.uint32(0x80000000), F32)
    sp = jnp.maximum(z, 0.0) + jnp.log2(1.0 + jnp.exp2(neg_abs))
    if mask is not None:
        sp = jnp.where(mask, sp, 0.0)
    hi = sp.astype(BF16)
    lo = (sp - hi.astype(F32)).astype(BF16)
    parts = [None] * nsub
    for n in reversed(range(nsub)):
        c = slice(n * LANES, (n + 1) * LANES)
        sums = _dot(jnp.concatenate([hi[:, c], lo[:, c]], axis=1), tri2)
        parts[n] = z[:, c] + sums[:, :LANES] + run
        run = run + sums[:, LANES:]
    a = jnp.exp2(jnp.concatenate(parts, axis=1))
    if mask is not None:
        a = jnp.where(mask, a, 0.0)
    return run, _dot(a.astype(BF16), v)


def _tri2():
    j = np.arange(2 * LANES)[:, None] % LANES
    s = np.arange(2 * LANES)[None, :]
    return jnp.asarray(np.where(np.where(s < LANES, j >= s, True), -1.0, 0.0), dtype=BF16)


def _key_norm_max(k_ref, kmax_ref):
    @pl.when(pl.program_id(2) == 0)
    def _():
        kmax_ref[...] = jnp.max(_row_norm(k_ref[...]), axis=0, keepdims=True)


def _fox_prompt_kernel(q_ref, k_ref, v_ref, b_ref, o_ref, kmax_ref, *, t):
    i = pl.program_id(2)
    _key_norm_max(k_ref, kmax_ref)
    q = q_ref[...]
    bound = _row_norm(q) * kmax_ref[...]

    def update(j, carry, mask):
        ks = pl.multiple_of(j * t, t)
        return _flash_update(q, k_ref[pl.ds(ks, t), :], v_ref[pl.ds(ks, t), :], b_ref[j], mask, carry)

    def visible(j, m):
        top = jnp.max(b_ref[jnp.maximum(j, 0)], axis=-1, keepdims=True)
        return jnp.max(bound + top - m) > EXP2_ZERO

    rows = lax.broadcasted_iota(jnp.int32, (t, t), 0)
    cols = lax.broadcasted_iota(jnp.int32, (t, t), 1)
    carry = update(i, _flash_init(t), cols <= rows)

    def body(c):
        carry = update(c[0], c[2:], None)
        return (c[0] - 1, visible(c[0] - 1, carry[0])) + carry

    c = lax.while_loop(lambda c: jnp.logical_and(c[0] >= 0, c[1]), body, (i - 1, visible(i - 1, carry[0])) + carry)
    o_ref[...] = (c[4] / c[3]).astype(BF16)


def _diff_prompt_kernel(q_ref, k_ref, v_ref, lam_ref, g_ref, li_ref, o_ref, *, tq, tk):
    i = pl.program_id(2)
    q2 = _stack_diff_q(q_ref[...])
    jd = (i * tq) // tk

    def step(start, width, carry, mask):
        ks = pl.multiple_of(start, width)
        return _flash_update(q2, k_ref[pl.ds(ks, width), :], v_ref[pl.ds(ks, width), :], None, mask, carry)

    wide = lambda j, c: step(j * tk, tk, c, None)
    carry = lax.fori_loop(0, jd // 2, lambda n, c: wide(2 * n + 1, wide(2 * n, c)), _flash_init(2 * tq))
    carry = lax.cond(jd % 2 == 1, lambda c: wide(jd - 1, c), lambda c: c, carry)
    if tk > tq:
        carry = lax.cond(i * tq > jd * tk, lambda c: step((i - 1) * tq, tq, c, None), lambda c: c, carry)
    rows = lax.broadcasted_iota(jnp.int32, (2 * tq, tq), 0) % tq
    cols = lax.broadcasted_iota(jnp.int32, (2 * tq, tq), 1)
    carry = step(i * tq, tq, carry, (cols // CHUNK) <= (rows // CHUNK))
    o_ref[...] = _diff_finish(carry, tq, lam_ref, li_ref, g_ref).astype(BF16)


def _sb_prompt_kernel(q_ref, k_ref, v_ref, tri_ref, o_ref, kmax_ref, run_ref, acc_ref, *, t, ts):
    i = pl.program_id(2)
    _key_norm_max(k_ref, kmax_ref)
    q = q_ref[...]
    tri2 = tri_ref[...]
    bound = _row_norm(q) * kmax_ref[...]
    rows = lax.broadcasted_iota(jnp.int32, (t, t), 0)
    cols = lax.broadcasted_iota(jnp.int32, (t, t), 1)

    def span(start, width, mask, run):
        ks = pl.multiple_of(start, width)
        run, pv = _sb_span(q, k_ref[pl.ds(ks, width), :], v_ref[pl.ds(ks, width), :], tri2, mask, run)
        run_ref[...] = run
        return pv

    def visible():
        return jnp.max(bound + run_ref[:, :1]) > EXP2_ZERO

    acc_ref[...] = span(i * t, t, cols < rows, jnp.zeros((t, LANES), F32))

    def body(c):
        acc_ref[...] += span(c[0] * ts, ts, None, run_ref[...])
        return c[0] - 1, visible()

    lax.while_loop(lambda c: jnp.logical_and(c[0] >= 0, c[1]), body, (i * (t // ts) - 1, visible()))
    o_ref[...] = acc_ref[...].astype(BF16)


def _band_prompt_kernel(q_ref, kp_ref, kc_ref, vp_ref, vc_ref, bias_ref, o_ref, *, t):
    i = pl.program_id(2)
    q = q_ref[...]
    sp = _dot_nt(q, kp_ref[...]) + bias_ref[:, :t]
    sp = jnp.where(i > 0, sp, NEG_INF)
    sc = _dot_nt(q, kc_ref[...]) + bias_ref[:, t:]
    m = jnp.maximum(jnp.max(sp, axis=-1, keepdims=True), jnp.max(sc, axis=-1, keepdims=True))
    pp = jnp.exp2(sp - m)
    pc = jnp.exp2(sc - m)
    l = jnp.sum(pp, axis=-1, keepdims=True) + jnp.sum(pc, axis=-1, keepdims=True)
    o = _dot(pp.astype(BF16), vp_ref[...]) + _dot(pc.astype(BF16), vc_ref[...])
    o_ref[...] = (o / l).astype(BF16)


def _col(group, comp, h):
    return group * 3 * HEADS + comp * HEADS + h


def _prompt_call(kern, qkv, group, b, seq, tq, extra_in, extra_specs, scratch, name):
    nq = seq // tq
    m = qkv.shape[0]
    in_specs = [
        pl.BlockSpec((tq, LANES), lambda bi, h, i: (bi * nq + i, _col(group, 0, h))),
        pl.BlockSpec((seq, LANES), lambda bi, h, i: (bi, _col(group, 1, h))),
        pl.BlockSpec((seq, LANES), lambda bi, h, i: (bi, _col(group, 2, h))),
    ] + extra_specs
    return pl.pallas_call(
        kern,
        grid=(b, HEADS, nq),
        in_specs=in_specs,
        out_specs=pl.BlockSpec((tq, LANES), lambda bi, h, i: (bi * nq + i, h)),
        out_shape=jax.ShapeDtypeStruct((m, GROUP_W), BF16),
        scratch_shapes=scratch,
        compiler_params=_cparams("parallel", "parallel", "arbitrary"),
        name=name,
    )(qkv, qkv, qkv, *extra_in)


def _fox_prompt(qkv, key_bias, b, seq, t):
    nk = seq // t
    kb = jnp.swapaxes(key_bias, 1, 2).reshape(b * HEADS, nk, 1, t)
    spec = pl.BlockSpec((None, nk, 1, t), lambda bi, h, i: (bi * HEADS + h, 0, 0, 0))
    kern = functools.partial(_fox_prompt_kernel, t=t)
    return _prompt_call(kern, qkv, 0, b, seq, t, [kb], [spec], [pltpu.VMEM((1, 1), F32)], "fox_prompt")


def _diff_prompt(qkv, lam, dng, li, layer, b, seq, tq, tk):
    specs = [
        pl.BlockSpec((None, 4, DIFF_HALF), lambda bi, h, i: (layer, 0, 0)),
        pl.BlockSpec((None, 1, HEAD_DIM), lambda bi, h, i: (layer, 0, 0)),
        pl.BlockSpec(memory_space=pltpu.SMEM),
    ]
    kern = functools.partial(_diff_prompt_kernel, tq=tq, tk=tk)
    return _prompt_call(kern, qkv, 2, b, seq, tq, [lam, dng, li], specs, [], "diff_prompt")


def _sb_prompt(qkv, tri2, b, seq, t):
    spec = pl.BlockSpec((2 * LANES, 2 * LANES), lambda bi, h, i: (0, 0))
    kern = functools.partial(_sb_prompt_kernel, t=t, ts=max(t // 2, LANES))
    scratch = [pltpu.VMEM((1, 1), F32), pltpu.VMEM((t, LANES), F32), pltpu.VMEM((t, HEAD_DIM), F32)]
    return _prompt_call(kern, qkv, 3, b, seq, t, [tri2], [spec], scratch, "sb_prompt")


def _band_prompt(qkv, table, layer, b, seq, t):
    nq = seq // t
    m = qkv.shape[0]
    cur = lambda comp: (lambda bi, h, i: (bi * nq + i, _col(1, comp, h)))
    prev = lambda comp: (lambda bi, h, i: (bi * nq + jnp.maximum(i - 1, 0), _col(1, comp, h)))
    return pl.pallas_call(
        functools.partial(_band_prompt_kernel, t=t),
        grid=(b, HEADS, nq),
        in_specs=[
            pl.BlockSpec((t, LANES), cur(0)),
            pl.BlockSpec((t, LANES), prev(1)),
            pl.BlockSpec((t, LANES), cur(1)),
            pl.BlockSpec((t, LANES), prev(2)),
            pl.BlockSpec((t, LANES), cur(2)),
            pl.BlockSpec((None, None, t, 2 * t), lambda bi, h, i: (layer, h, 0, 0)),
        ],
        out_specs=pl.BlockSpec((t, LANES), lambda bi, h, i: (bi * nq + i, h)),
        out_shape=jax.ShapeDtypeStruct((m, GROUP_W), BF16),
        compiler_params=_cparams("parallel", "parallel", "arbitrary"),
        name="band_prompt",
    )(qkv, qkv, qkv, qkv, qkv, table)


def _cached_head(c_ref, h):
    rows = c_ref.shape[0] // HEADS
    return c_ref[pl.ds(h, rows, stride=HEADS), :].astype(BF16)


def _head_cols(h):
    return slice(h * LANES, (h + 1) * LANES)


def _fox_sample_kernel(q_ref, kn_ref, vn_ref, kc_ref, vc_ref, bc_ref, bn_ref, o_ref, *, ts):
    rows = lax.broadcasted_iota(jnp.int32, (ts, ts), 0)
    cols = lax.broadcasted_iota(jnp.int32, (ts, ts), 1)
    for h in range(HEADS):
        c = _head_cols(h)
        q = q_ref[:, c]
        carry = _flash_update(q, _cached_head(kc_ref, h), _cached_head(vc_ref, h), bc_ref[h], None, _flash_init(ts))
        _, l, acc = _flash_update(q, kn_ref[:, c], vn_ref[:, c], bn_ref[h], cols <= rows, carry)
        o_ref[:, c] = (acc / l).astype(BF16)


def _band_sample_kernel(q_ref, kn_ref, vn_ref, kc_ref, vc_ref, bias_ref, o_ref, *, ts, lb):
    for h in range(HEADS):
        c = _head_cols(h)
        q = q_ref[:, c]
        carry = _flash_update(q, _cached_head(kc_ref, h), _cached_head(vc_ref, h), bias_ref[h, :, :lb], None,
                              _flash_init(ts))
        _, l, acc = _flash_update(q, kn_ref[:, c], vn_ref[:, c], bias_ref[h, :, lb:], None, carry)
        o_ref[:, c] = (acc / l).astype(BF16)


def _diff_sample_kernel(q_ref, kn_ref, vn_ref, kc_ref, vc_ref, lam_ref, g_ref, li_ref, o_ref, *, ts):
    for h in range(HEADS):
        c = _head_cols(h)
        q2 = _stack_diff_q(q_ref[:, c])
        carry = _flash_update(q2, _cached_head(kc_ref, h), _cached_head(vc_ref, h), None, None,
                              _flash_init(2 * ts))
        carry = _flash_update(q2, kn_ref[:, c], vn_ref[:, c], None, None, carry)
        o_ref[:, c] = _diff_finish(carry, ts, lam_ref, li_ref, g_ref).astype(BF16)


def _sb_sample_kernel(q_ref, kn_ref, vn_ref, kc_ref, vc_ref, tri_ref, o_ref, *, ts):
    tri2 = tri_ref[...]
    pad = jnp.zeros((LANES - ts, HEAD_DIM), BF16)
    rows = lax.broadcasted_iota(jnp.int32, (ts, LANES), 0)
    cols = lax.broadcasted_iota(jnp.int32, (ts, LANES), 1)
    for h in range(HEADS):
        c = _head_cols(h)
        q = q_ref[:, c]
        run, acc = _sb_span(q, jnp.concatenate([kn_ref[:, c], pad], axis=0),
                            jnp.concatenate([vn_ref[:, c], pad], axis=0), tri2, cols < rows,
                            jnp.zeros((ts, LANES), F32))
        _, pv = _sb_span(q, _cached_head(kc_ref, h), _cached_head(vc_ref, h), tri2, None, run)
        o_ref[:, c] = (acc + pv).astype(BF16)


def _sample_call(kern, qkv, group, cache_k, cache_v, layer, bs, ts, extra_in, extra_specs, name):
    m = qkv.shape[0]
    rows = cache_k.shape[1]
    new = lambda comp: pl.BlockSpec((ts, GROUP_W), lambda b: (b, group * 3 + comp))
    cache = pl.BlockSpec((None, rows, LANES), lambda b: (layer * bs + b, 0, 0))
    return pl.pallas_call(
        kern,
        grid=(bs,),
        in_specs=[new(0), new(1), new(2), cache, cache] + extra_specs,
        out_specs=pl.BlockSpec((ts, GROUP_W), lambda b: (b, 0)),
        out_shape=jax.ShapeDtypeStruct((m, GROUP_W), BF16),
        compiler_params=_cparams("parallel"),
        name=name,
    )(qkv, qkv, qkv, cache_k, cache_v, *extra_in)


def _rope_tables(pos):
    inv = ROPE_THETA ** (-jnp.arange(ROPE_HALF, dtype=F32) / ROPE_HALF)
    ang = pos.astype(F32)[:, None] * inv[None, :]
    cos, sin = jnp.cos(ang), jnp.sin(ang)
    return jnp.tile(cos, (1, 4)), jnp.concatenate([-sin, sin, -sin, sin], axis=1)


def _band_table(rel, nrows, ncols, q_off):
    period = nrows + ncols
    off = np.arange(period)
    off = np.where(off < ncols, off, off - period)
    vec = rel.astype(F32)[:, :, np.clip(q_off - off, -REL_CLIP, REL_CLIP) + REL_CLIP] * LOG2E
    toep = jnp.tile(vec, (1, 1, nrows))[:, :, :nrows * (period - 1)]
    toep = toep.reshape(rel.shape[0], rel.shape[1], nrows, period - 1)[:, :, :, :ncols]
    qc = (q_off + np.arange(nrows)[:, None]) // CHUNK
    kc = np.arange(ncols)[None, :] // CHUNK
    vis = (kc <= qc) & (kc >= qc - BAND_CHUNKS)
    return jnp.where(jnp.asarray(vis), toep, NEG_INF)


def _row_tile(m, want):
    t = min(m, want)
    assert m % t == 0
    return t


def kernel(x_prompt, x_sample, cache_a_k, cache_a_v, cache_a_logf, cache_b_k, cache_b_v, cache_c_k, cache_c_v, cache_d_k, cache_d_v, attn_norm_g, w_in, fox_b, band_rel_bias, diff_lambda, diff_norm_g, w_out, mlp_norm_g, w_up, w_down, final_norm_g):
    b, seq, d = x_prompt.shape
    bs, ts, _ = x_sample.shape
    depth = w_in.shape[0]
    past = cache_a_k.shape[2]
    lb = cache_b_k.shape[2]
    lbp = min(BAND_PAST, seq)
    mp, ms = b * seq, bs * ts
    assert w_in.shape[2] == QKV_W + HEADS and d == N_GROUPS * GROUP_W
    assert ts == CHUNK and past % LANES == 0 and past % CHUNK == 0 and lb == BAND_PAST

    tm_p = _row_tile(seq, 1024)
    tm_s = _row_tile(ms, 1024)
    assert tm_s % ts == 0 and tm_p >= lbp
    tm_res_p, tm_res_s = _row_tile(mp, 512), _row_tile(ms, 512)
    tm_up_p, tm_up_s = _row_tile(mp, 1024), _row_tile(ms, 1024)
    tn_up = _row_tile(w_up.shape[2], 1024)
    tk_down = _row_tile(w_up.shape[2], 2048)
    t_attn = _row_tile(seq, 512)
    t_diff_k = _row_tile(seq, 1024)
    assert t_attn >= BAND_PAST and t_diff_k in (t_attn, 2 * t_attn)

    w_qkv = w_in[:, :, :QKV_W].astype(BF16)
    w_logit = jnp.pad(w_in[:, :, QKV_W:], ((0, 0), (0, 0), (0, LANES - HEADS))).astype(BF16)
    fox_b_pad = jnp.pad(fox_b.astype(F32), ((0, 0), (0, LANES - HEADS)))[:, None, :]
    w_out_b, w_up_b, w_down_b = w_out.astype(BF16), w_up.astype(BF16), w_down.astype(BF16)
    attn_g = attn_norm_g.astype(F32)[:, None, :]
    mlp_g = mlp_norm_g.astype(F32)[:, None, :]
    dng = diff_norm_g.astype(F32)[:, None, :]
    lam_rows = diff_lambda.astype(F32)
    cos_p, sin_p = _rope_tables(jnp.arange(seq))
    cos_s, sin_s = _rope_tables(past + jnp.arange(ts))
    cos_s, sin_s = jnp.tile(cos_s, (tm_s // ts, 1)), jnp.tile(sin_s, (tm_s // ts, 1))
    table_p = _band_table(band_rel_bias, t_attn, 2 * t_attn, t_attn)
    table_s = _band_table(band_rel_bias, ts, lb + ts, lb)
    tri2 = _tri2()

    flat = lambda c: c.reshape(depth * bs, c.shape[2] * HEADS, HEAD_DIM)
    ca_k, ca_v, cb_k, cb_v = flat(cache_a_k), flat(cache_a_v), flat(cache_b_k), flat(cache_b_v)
    cc_k, cc_v, cd_k, cd_v = flat(cache_c_k), flat(cache_c_v), flat(cache_d_k), flat(cache_d_v)

    next_g = jnp.concatenate([attn_g[1:], final_norm_g.astype(F32)[None, None, :]], axis=0)
    xp = x_prompt.reshape(mp, d)
    xs = x_sample.reshape(ms, d)
    hp = _norm(xp, attn_g, 0, tm_res_p)
    hs = _norm(xs, attn_g, 0, tm_res_s)
    p_st = s_st = None
    p_logf, s_logf, s_bk, s_bv = [], [], [], []
    for l in range(depth):
        lam_init = 0.8 - 0.6 * float(np.exp(-0.3 * l))
        li = jnp.asarray([lam_init], F32)

        h_dtype = BF16 if l + 1 < depth else F32
        qkv, logf, p_st = _proj(hp, w_qkv, w_logit, fox_b_pad, cos_p, sin_p, l, depth, tm_p, lbp, p_st)
        logf = logf[:, :HEADS].reshape(b, seq, HEADS)
        p_logf.append(logf)
        key_bias = jnp.cumsum(logf, axis=1) * (-LOG2E)
        oa = _fox_prompt(qkv, key_bias, b, seq, t_attn)
        ob = _band_prompt(qkv, table_p, l, b, seq, t_attn)
        oc = _diff_prompt(qkv, lam_rows, dng, li, l, b, seq, t_attn, t_diff_k)
        od = _sb_prompt(qkv, tri2, b, seq, t_attn)
        xp, hm = _out_proj(xp, [oa, ob, oc, od], w_out_b, mlp_g, l, tm_res_p)
        u = _up(hm, w_up_b, l, tm_up_p, tn_up)
        xp, hp = _down(xp, u, w_down_b, next_g, l, h_dtype, tm_res_p, tk_down)

        qkv, logf, s_st = _proj(hs, w_qkv, w_logit, fox_b_pad, cos_s, sin_s, l, depth, tm_s, 0, s_st)
        logf = logf[:, :HEADS].reshape(bs, ts, HEADS)
        s_logf.append(logf)
        s_bk.append(jnp.concatenate([cache_b_k[l][:, ts:], s_st[2].reshape(bs, ts, HEADS, HEAD_DIM)], axis=1))
        s_bv.append(jnp.concatenate([cache_b_v[l][:, ts:], s_st[3].reshape(bs, ts, HEADS, HEAD_DIM)], axis=1))
        cum_f = jnp.cumsum(jnp.concatenate([cache_a_logf[l].astype(F32), logf], axis=1), axis=1)
        kb = jnp.swapaxes(cum_f * (-LOG2E), 1, 2).reshape(bs, HEADS, 1, past + ts)
        kb_specs = [pl.BlockSpec((None, HEADS, 1, past), lambda bi: (bi, 0, 0, 0)),
                    pl.BlockSpec((None, HEADS, 1, ts), lambda bi: (bi, 0, 0, 0))]
        oa = _sample_call(functools.partial(_fox_sample_kernel, ts=ts), qkv, 0, ca_k, ca_v, l, bs, ts,
                          [kb[..., :past], kb[..., past:]], kb_specs, "fox_sample")
        ob = _sample_call(functools.partial(_band_sample_kernel, ts=ts, lb=lb), qkv, 1, cb_k, cb_v,
                          l, bs, ts, [table_s],
                          [pl.BlockSpec((None, HEADS, ts, lb + ts), lambda bi: (l, 0, 0, 0))], "band_sample")
        oc = _sample_call(functools.partial(_diff_sample_kernel, ts=ts), qkv, 2, cc_k, cc_v, l, bs, ts,
                          [lam_rows, dng, li],
                          [pl.BlockSpec((None, 4, DIFF_HALF), lambda bi: (l, 0, 0)),
                           pl.BlockSpec((None, 1, HEAD_DIM), lambda bi: (l, 0, 0)),
                           pl.BlockSpec(memory_space=pltpu.SMEM)], "diff_sample")
        od = _sample_call(functools.partial(_sb_sample_kernel, ts=ts), qkv, 3, cd_k, cd_v,
                          l, bs, ts, [tri2], [pl.BlockSpec((2 * LANES, 2 * LANES), lambda bi: (0, 0))],
                          "sb_sample")
        xs, hm = _out_proj(xs, [oa, ob, oc, od], w_out_b, mlp_g, l, tm_res_s)
        u = _up(hm, w_up_b, l, tm_up_s, tn_up)
        xs, hs = _down(xs, u, w_down_b, next_g, l, h_dtype, tm_res_s, tk_down)

    y_prompt = hp.reshape(b, seq, d)
    y_sample = hs.reshape(bs, ts, d)
    pst = lambda n, rows: p_st[n].reshape(depth, b, rows, HEADS, HEAD_DIM)
    sst = lambda n: s_st[n].reshape(depth, bs, ts, HEADS, HEAD_DIM)
    return (y_prompt, y_sample,
            pst(0, seq), pst(1, seq), jnp.stack(p_logf), pst(2, lbp), pst(3, lbp),
            pst(4, seq), pst(5, seq), pst(6, seq), pst(7, seq),
            sst(0), sst(1), jnp.stack(s_logf), jnp.stack(s_bk), jnp.stack(s_bv),
            sst(4), sst(5), sst(6), sst(7))
```

```python
import functools
import math

import numpy as np
import jax
import jax.numpy as jnp
from jax import lax
from jax.experimental import pallas as pl
from jax.experimental.pallas import tpu as pltpu

F32 = jnp.float32
BF16 = jnp.bfloat16

LANES = 128
HEAD_DIM = 128
HEADS = 4
GROUP_W = HEADS * HEAD_DIM
N_GROUPS = 4
QKV_W = 3 * N_GROUPS * GROUP_W
CHUNK = 64
BAND_CHUNKS = 8
BAND_PAST = BAND_CHUNKS * CHUNK
REL_CLIP = 128
DIFF_HALF = HEAD_DIM // 2
ROPE_HALF = DIFF_HALF // 2
ROPE_THETA = 10000.0
EPS = 1e-6
NEG_INF = -1e30
LOG2E = math.log2(math.e)
SCALE = HEAD_DIM ** -0.5
VMEM_LIMIT = 56 * 1024 * 1024

EXP2_ZERO = -160.0

TILE_CQ = 2 * 3 + 0
TILE_CK = 2 * 3 + 1
Q_SCALE = SCALE * LOG2E
Q_SCALE_DIFF = DIFF_HALF ** -0.5 * LOG2E


def _cparams(*sem):
    return pltpu.CompilerParams(dimension_semantics=sem, vmem_limit_bytes=VMEM_LIMIT)


def _dot(a, b):
    return jnp.dot(a, b, preferred_element_type=F32)


def _dot_nt(a, b):
    return lax.dot_general(a, b, (((1,), (1,)), ((), ())), preferred_element_type=F32)


def _rmsnorm_rows(x, g):
    return x * lax.rsqrt(jnp.mean(x * x, axis=-1, keepdims=True) + EPS) * g


def _log_sigmoid(z):
    return jnp.minimum(z, 0.0) - jnp.log1p(jnp.exp(-jnp.abs(z)))


def _row_norm(x):
    xf = x.astype(F32)
    return jnp.sqrt(jnp.sum(xf * xf, axis=-1, keepdims=True))


def _rope_tile(y, cos, sin):
    lane = lax.broadcasted_iota(jnp.int32, (1, LANES), 1)
    first = (lane % DIFF_HALF) < ROPE_HALF
    outs = []
    for h in range(HEADS):
        x = y[:, h * LANES:(h + 1) * LANES]
        partner = jnp.where(first, pltpu.roll(x, LANES - ROPE_HALF, 1), pltpu.roll(x, ROPE_HALF, 1))
        outs.append(x * cos + partner * sin)
    return jnp.concatenate(outs, axis=1)


def _store_heads(st_ref, val):
    rows = val.shape[0]
    for h in range(HEADS):
        st_ref[pl.ds(h, rows, stride=HEADS), :] = val[:, h * LANES:(h + 1) * LANES]


def _proj_kernel(*refs, n_alias, nper, tail_rows):
    h_ref, w_ref, wl_ref, fb_ref, cos_ref, sin_ref = refs[:6]
    qkv_ref, logf_ref = refs[6 + n_alias:8 + n_alias]
    st_refs = refs[8 + n_alias:16 + n_alias]
    i, j = pl.program_id(0), pl.program_id(1)
    tm = h_ref.shape[0]

    @pl.when(j == 0)
    def _():
        logf_ref[...] = _log_sigmoid(_dot(h_ref[...], wl_ref[...]) + fb_ref[...])

    acc = _dot(h_ref[...], w_ref[...])
    is_q = j % 3 == 0

    @pl.when(j == TILE_CQ)
    def _():
        qkv_ref[...] = (_rope_tile(acc, cos_ref[...], sin_ref[...]) * Q_SCALE_DIFF).astype(BF16)

    @pl.when(j == TILE_CK)
    def _():
        y = _rope_tile(acc, cos_ref[...], sin_ref[...])
        qkv_ref[...] = y.astype(BF16)
        _store_heads(st_refs[4], y)

    @pl.when(jnp.logical_and(is_q, j != TILE_CQ))
    def _():
        qkv_ref[...] = (acc * Q_SCALE).astype(BF16)

    @pl.when(jnp.logical_and(jnp.logical_not(is_q), j != TILE_CK))
    def _():
        qkv_ref[...] = acc.astype(BF16)

    for n, st_ref in enumerate(st_refs):
        if n == 4:
            continue
        cond = j == 3 * (n // 2) + 1 + n % 2
        if tail_rows and n in (2, 3):
            @pl.when(jnp.logical_and(cond, i % nper == nper - 1))
            def _(st_ref=st_ref):
                _store_heads(st_ref, acc[tm - tail_rows:, :])
        else:
            @pl.when(cond)
            def _(st_ref=st_ref):
                _store_heads(st_ref, acc)


def _proj(h, w_qkv, w_logit, fox_b, cos_t, sin_t, layer, depth, tm, tail_rows, bufs):
    m, d = h.shape
    tn = GROUP_W
    nt = m // tm
    nper = cos_t.shape[0] // tm
    full = pl.BlockSpec((tm * HEADS, LANES), lambda i, j: (layer * nt + i, 0))
    full_shape = jax.ShapeDtypeStruct((depth * m * HEADS, LANES), F32)
    bspec, bshape = full, full_shape
    if tail_rows:
        nstream = nt // nper
        bspec = pl.BlockSpec((tail_rows * HEADS, LANES), lambda i, j: (layer * nstream + i // nper, 0))
        bshape = jax.ShapeDtypeStruct((depth * nstream * tail_rows * HEADS, LANES), F32)
    st_specs = [full, full, bspec, bspec, full, full, full, full]
    st_shapes = [full_shape, full_shape, bshape, bshape, full_shape, full_shape, full_shape, full_shape]
    alias_in = list(bufs) if bufs is not None else []
    aliases = {6 + n: 2 + n for n in range(len(alias_in))}
    outs = pl.pallas_call(
        functools.partial(_proj_kernel, n_alias=len(alias_in), nper=nper, tail_rows=tail_rows),
        grid=(nt, QKV_W // tn),
        in_specs=[
            pl.BlockSpec((tm, d), lambda i, j: (i, 0)),
            pl.BlockSpec((None, d, tn), lambda i, j: (layer, 0, j)),
            pl.BlockSpec((None, d, LANES), lambda i, j: (layer, 0, 0)),
            pl.BlockSpec((None, 1, LANES), lambda i, j: (layer, 0, 0)),
            pl.BlockSpec((tm, LANES), lambda i, j: (i % nper, 0)),
            pl.BlockSpec((tm, LANES), lambda i, j: (i % nper, 0)),
        ] + [pl.BlockSpec(memory_space=pl.ANY)] * len(alias_in),
        out_specs=[
            pl.BlockSpec((tm, tn), lambda i, j: (i, j)),
            pl.BlockSpec((tm, LANES), lambda i, j: (i, 0)),
        ] + st_specs,
        out_shape=[
            jax.ShapeDtypeStruct((m, QKV_W), BF16),
            jax.ShapeDtypeStruct((m, LANES), F32),
        ] + st_shapes,
        input_output_aliases=aliases,
        compiler_params=_cparams("parallel", "arbitrary"),
        name="proj",
    )(h, w_qkv, w_logit, fox_b, cos_t, sin_t, *alias_in)
    return outs[0], outs[1], list(outs[2:])


def _norm_kernel(x_ref, g_ref, h_ref):
    h_ref[...] = _rmsnorm_rows(x_ref[...], g_ref[...]).astype(h_ref.dtype)


def _norm(x, g, layer, tm):
    m, d = x.shape
    return pl.pallas_call(
        _norm_kernel,
        grid=(m // tm,),
        in_specs=[pl.BlockSpec((tm, d), lambda i: (i, 0)), pl.BlockSpec((None, 1, d), lambda i: (layer, 0, 0))],
        out_specs=pl.BlockSpec((tm, d), lambda i: (i, 0)),
        out_shape=jax.ShapeDtypeStruct((m, d), BF16),
        compiler_params=_cparams("parallel"),
        name="norm",
    )(x, g)


def _out_proj_kernel(x_ref, oa_ref, ob_ref, oc_ref, od_ref, w_ref, g_ref, xo_ref, h_ref):
    acc = x_ref[...]
    for n, o_ref in enumerate((oa_ref, ob_ref, oc_ref, od_ref)):
        acc = acc + _dot(o_ref[...], w_ref[n * GROUP_W:(n + 1) * GROUP_W, :])
    xo_ref[...] = acc
    h_ref[...] = _rmsnorm_rows(acc, g_ref[...]).astype(BF16)


def _out_proj(x, o_list, w, g, layer, tm):
    m, d = x.shape
    row = lambda width: pl.BlockSpec((tm, width), lambda i: (i, 0))
    return pl.pallas_call(
        _out_proj_kernel,
        grid=(m // tm,),
        in_specs=[row(d)] + [row(GROUP_W)] * N_GROUPS + [
            pl.BlockSpec((None, N_GROUPS * GROUP_W, d), lambda i: (layer, 0, 0)),
            pl.BlockSpec((None, 1, d), lambda i: (layer, 0, 0)),
        ],
        out_specs=[row(d), row(d)],
        out_shape=[jax.ShapeDtypeStruct((m, d), F32), jax.ShapeDtypeStruct((m, d), BF16)],
        compiler_params=_cparams("parallel"),
        name="out_proj",
    )(x, *o_list, w, g)


def _up_kernel(h_ref, w_ref, u_ref):
    r = jnp.maximum(_dot(h_ref[...], w_ref[...]), 0.0)
    u_ref[...] = (r * r).astype(BF16)


def _up(h, w_up, layer, tm, tn):
    m, d = h.shape
    f = w_up.shape[2]
    return pl.pallas_call(
        _up_kernel,
        grid=(m // tm, f // tn),
        in_specs=[
            pl.BlockSpec((tm, d), lambda i, j: (i, 0)),
            pl.BlockSpec((None, d, tn), lambda i, j: (layer, 0, j)),
        ],
        out_specs=pl.BlockSpec((tm, tn), lambda i, j: (i, j)),
        out_shape=jax.ShapeDtypeStruct((m, f), BF16),
        compiler_params=_cparams("parallel", "arbitrary"),
        name="mlp_up",
    )(h, w_up)


def _down_kernel(x_ref, u_ref, w_ref, g_ref, xo_ref, h_ref):
    k = pl.program_id(1)

    @pl.when(k == 0)
    def _():
        xo_ref[...] = x_ref[...] + _dot(u_ref[...], w_ref[...])

    @pl.when(k > 0)
    def _():
        xo_ref[...] += _dot(u_ref[...], w_ref[...])

    @pl.when(k == pl.num_programs(1) - 1)
    def _():
        h_ref[...] = _rmsnorm_rows(xo_ref[...], g_ref[...]).astype(h_ref.dtype)


def _down(x, u, w, g, layer, h_dtype, tm, tk):
    m, d = x.shape
    f = u.shape[1]
    return pl.pallas_call(
        _down_kernel,
        grid=(m // tm, f // tk),
        in_specs=[
            pl.BlockSpec((tm, d), lambda i, k: (i, 0)),
            pl.BlockSpec((tm, tk), lambda i, k: (i, k)),
            pl.BlockSpec((None, tk, d), lambda i, k: (layer, k, 0)),
            pl.BlockSpec((None, 1, d), lambda i, k: (layer, 0, 0)),
        ],
        out_specs=[pl.BlockSpec((tm, d), lambda i, k: (i, 0)), pl.BlockSpec((tm, d), lambda i, k: (i, 0))],
        out_shape=[jax.ShapeDtypeStruct((m, d), F32), jax.ShapeDtypeStruct((m, d), h_dtype)],
        compiler_params=_cparams("parallel", "arbitrary"),
        name="mlp_down",
    )(x, u, w, g)


def _flash_init(rows):
    return (jnp.full((rows, 1), NEG_INF, F32), jnp.zeros((rows, 1), F32), jnp.zeros((rows, HEAD_DIM), F32))


def _flash_update(q, k, v, bias, mask, carry):
    m, l, acc = carry
    s = _dot_nt(q, k)
    if bias is not None:
        s = s + bias
    if mask is not None:
        s = jnp.where(mask, s, NEG_INF)
    m_new = jnp.maximum(m, jnp.max(s, axis=-1, keepdims=True))
    alpha = jnp.exp2(m - m_new)
    p = jnp.exp2(s - m_new)
    l = alpha * l + jnp.sum(p, axis=-1, keepdims=True)
    acc = alpha * acc + _dot(p.astype(BF16), v)
    return m_new, l, acc


def _stack_diff_q(q):
    lane = lax.broadcasted_iota(jnp.int32, q.shape, 1)
    zero = jnp.zeros_like(q)
    return jnp.concatenate([jnp.where(lane < DIFF_HALF, q, zero), jnp.where(lane >= DIFF_HALF, q, zero)], axis=0)


def _diff_finish(carry, tq, lam_ref, li_ref, g_ref):
    _, l, acc = carry
    o = acc / l
    lv = lam_ref[...]
    lam = (jnp.exp(jnp.sum(lv[0:1] * lv[1:2], axis=-1, keepdims=True))
           - jnp.exp(jnp.sum(lv[2:3] * lv[3:4], axis=-1, keepdims=True)) + li_ref[0])
    d = o[:tq] - lam * o[tq:]
    return _rmsnorm_rows(d, g_ref[...]) * (1.0 - li_ref[0])


def _sb_span(q, k, v, tri2, mask, run):
    nsub = k.shape[0] // LANES
    z = _dot_nt(q, k)
    neg_abs = pltpu.bitcast(pltpu.bitcast(z, jnp.uint32) | jnp.uint32(0x80000000), F32)
    sp = jnp.maximum(z, 0.0) + jnp.log2(1.0 + jnp.exp2(neg_abs))
    if mask is not None:
        sp = jnp.where(mask, sp, 0.0)
    hi = sp.astype(BF16)
    lo = (sp - hi.astype(F32)).astype(BF16)
    parts = [None] * nsub
    for n in reversed(range(nsub)):
        c = slice(n * LANES, (n + 1) * LANES)
        sums = _dot(jnp.concatenate([hi[:, c], lo[:, c]], axis=1), tri2)
        parts[n] = z[:, c] + sums[:, :LANES] + run
        run = run + sums[:, LANES:]
    a = jnp.exp2(jnp.concatenate(parts, axis=1))
    if mask is not None:
        a = jnp.where(mask, a, 0.0)
    return run, _dot(a.astype(BF16), v)


def _tri2():
    j = np.arange(2 * LANES)[:, None] % LANES
    s = np.arange(2 * LANES)[None, :]
    return jnp.asarray(np.where(np.where(s < LANES, j >= s, True), -1.0, 0.0), dtype=BF16)


def _key_norm_max(k_ref, kmax_ref):
    @pl.when(pl.program_id(2) == 0)
    def _():
        kmax_ref[...] = jnp.max(_row_norm(k_ref[...]), axis=0, keepdims=True)


def _fox_prompt_kernel(q_ref, k_ref, v_ref, b_ref, o_ref, kmax_ref, *, t):
    i = pl.program_id(2)
    _key_norm_max(k_ref, kmax_ref)
    q = q_ref[...]
    bound = _row_norm(q) * kmax_ref[...]

    def update(j, carry, mask):
        ks = pl.multiple_of(j * t, t)
        return _flash_update(q, k_ref[pl.ds(ks, t), :], v_ref[pl.ds(ks, t), :], b_ref[j], mask, carry)

    def visible(j, m):
        top = jnp.max(b_ref[jnp.maximum(j, 0)], axis=-1, keepdims=True)
        return jnp.max(bound + top - m) > EXP2_ZERO

    rows = lax.broadcasted_iota(jnp.int32, (t, t), 0)
    cols = lax.broadcasted_iota(jnp.int32, (t, t), 1)
    carry = update(i, _flash_init(t), cols <= rows)

    def body(c):
        carry = update(c[0], c[2:], None)
        return (c[0] - 1, visible(c[0] - 1, carry[0])) + carry

    c = lax.while_loop(lambda c: jnp.logical_and(c[0] >= 0, c[1]), body, (i - 1, visible(i - 1, carry[0])) + carry)
    o_ref[...] = (c[4] / c[3]).astype(BF16)


def _diff_prompt_kernel(q_ref, k_ref, v_ref, lam_ref, g_ref, li_ref, o_ref, *, tq, tk):
    i = pl.program_id(2)
    q2 = _stack_diff_q(q_ref[...])
    jd = (i * tq) // tk

    def step(start, width, carry, mask):
        ks = pl.multiple_of(start, width)
        return _flash_update(q2, k_ref[pl.ds(ks, width), :], v_ref[pl.ds(ks, width), :], None, mask, carry)

    wide = lambda j, c: step(j * tk, tk, c, None)
    carry = lax.fori_loop(0, jd // 2, lambda n, c: wide(2 * n + 1, wide(2 * n, c)), _flash_init(2 * tq))
    carry = lax.cond(jd % 2 == 1, lambda c: wide(jd - 1, c), lambda c: c, carry)
    if tk > tq:
        carry = lax.cond(i * tq > jd * tk, lambda c: step((i - 1) * tq, tq, c, None), lambda c: c, carry)
    rows = lax.broadcasted_iota(jnp.int32, (2 * tq, tq), 0) % tq
    cols = lax.broadcasted_iota(jnp.int32, (2 * tq, tq), 1)
    carry = step(i * tq, tq, carry, (cols // CHUNK) <= (rows // CHUNK))
    o_ref[...] = _diff_finish(carry, tq, lam_ref, li_ref, g_ref).astype(BF16)


def _sb_prompt_kernel(q_ref, k_ref, v_ref, tri_ref, o_ref, kmax_ref, run_ref, acc_ref, *, t, ts):
    i = pl.program_id(2)
    _key_norm_max(k_ref, kmax_ref)
    q = q_ref[...]
    tri2 = tri_ref[...]
    bound = _row_norm(q) * kmax_ref[...]
    rows = lax.broadcasted_iota(jnp.int32, (t, t), 0)
    cols = lax.broadcasted_iota(jnp.int32, (t, t), 1)

    def span(start, width, mask, run):
        ks = pl.multiple_of(start, width)
        run, pv = _sb_span(q, k_ref[pl.ds(ks, width), :], v_ref[pl.ds(ks, width), :], tri2, mask, run)
        run_ref[...] = run
        return pv

    def visible():
        return jnp.max(bound + run_ref[:, :1]) > EXP2_ZERO

    acc_ref[...] = span(i * t, t, cols < rows, jnp.zeros((t, LANES), F32))

    def body(c):
        acc_ref[...] += span(c[0] * ts, ts, None, run_ref[...])
        return c[0] - 1, visible()

    lax.while_loop(lambda c: jnp.logical_and(c[0] >= 0, c[1]), body, (i * (t // ts) - 1, visible()))
    o_ref[...] = acc_ref[...].astype(BF16)


def _band_prompt_kernel(q_ref, kp_ref, kc_ref, vp_ref, vc_ref, bias_ref, o_ref, *, t):
    i = pl.program_id(2)
    q = q_ref[...]
    sp = _dot_nt(q, kp_ref[...]) + bias_ref[:, :t]
    sp = jnp.where(i > 0, sp, NEG_INF)
    sc = _dot_nt(q, kc_ref[...]) + bias_ref[:, t:]
    m = jnp.maximum(jnp.max(sp, axis=-1, keepdims=True), jnp.max(sc, axis=-1, keepdims=True))
    pp = jnp.exp2(sp - m)
    pc = jnp.exp2(sc - m)
    l = jnp.sum(pp, axis=-1, keepdims=True) + jnp.sum(pc, axis=-1, keepdims=True)
    o = _dot(pp.astype(BF16), vp_ref[...]) + _dot(pc.astype(BF16), vc_ref[...])
    o_ref[...] = (o / l).astype(BF16)


def _col(group, comp, h):
    return group * 3 * HEADS + comp * HEADS + h


def _prompt_call(kern, qkv, group, b, seq, tq, extra_in, extra_specs, scratch, name):
    nq = seq // tq
    m = qkv.shape[0]
    in_specs = [
        pl.BlockSpec((tq, LANES), lambda bi, h, i: (bi * nq + i, _col(group, 0, h))),
        pl.BlockSpec((seq, LANES), lambda bi, h, i: (bi, _col(group, 1, h))),
        pl.BlockSpec((seq, LANES), lambda bi, h, i: (bi, _col(group, 2, h))),
    ] + extra_specs
    return pl.pallas_call(
        kern,
        grid=(b, HEADS, nq),
        in_specs=in_specs,
        out_specs=pl.BlockSpec((tq, LANES), lambda bi, h, i: (bi * nq + i, h)),
        out_shape=jax.ShapeDtypeStruct((m, GROUP_W), BF16),
        scratch_shapes=scratch,
        compiler_params=_cparams("parallel", "parallel", "arbitrary"),
        name=name,
    )(qkv, qkv, qkv, *extra_in)


def _fox_prompt(qkv, key_bias, b, seq, t):
    nk = seq // t
    kb = jnp.swapaxes(key_bias, 1, 2).reshape(b * HEADS, nk, 1, t)
    spec = pl.BlockSpec((None, nk, 1, t), lambda bi, h, i: (bi * HEADS + h, 0, 0, 0))
    kern = functools.partial(_fox_prompt_kernel, t=t)
    return _prompt_call(kern, qkv, 0, b, seq, t, [kb], [spec], [pltpu.VMEM((1, 1), F32)], "fox_prompt")


def _diff_prompt(qkv, lam, dng, li, layer, b, seq, tq, tk):
    specs = [
        pl.BlockSpec((None, 4, DIFF_HALF), lambda bi, h, i: (layer, 0, 0)),
        pl.BlockSpec((None, 1, HEAD_DIM), lambda bi, h, i: (layer, 0, 0)),
        pl.BlockSpec(memory_space=pltpu.SMEM),
    ]
    kern = functools.partial(_diff_prompt_kernel, tq=tq, tk=tk)
    return _prompt_call(kern, qkv, 2, b, seq, tq, [lam, dng, li], specs, [], "diff_prompt")


def _sb_prompt(qkv, tri2, b, seq, t):
    spec = pl.BlockSpec((2 * LANES, 2 * LANES), lambda bi, h, i: (0, 0))
    kern = functools.partial(_sb_prompt_kernel, t=t, ts=max(t // 2, LANES))
    scratch = [pltpu.VMEM((1, 1), F32), pltpu.VMEM((t, LANES), F32), pltpu.VMEM((t, HEAD_DIM), F32)]
    return _prompt_call(kern, qkv, 3, b, seq, t, [tri2], [spec], scratch, "sb_prompt")


def _band_prompt(qkv, table, layer, b, seq, t):
    nq = seq // t
    m = qkv.shape[0]
    cur = lambda comp: (lambda bi, h, i: (bi * nq + i, _col(1, comp, h)))
    prev = lambda comp: (lambda bi, h, i: (bi * nq + jnp.maximum(i - 1, 0), _col(1, comp, h)))
    return pl.pallas_call(
        functools.partial(_band_prompt_kernel, t=t),
        grid=(b, HEADS, nq),
        in_specs=[
            pl.BlockSpec((t, LANES), cur(0)),
            pl.BlockSpec((t, LANES), prev(1)),
            pl.BlockSpec((t, LANES), cur(1)),
            pl.BlockSpec((t, LANES), prev(2)),
            pl.BlockSpec((t, LANES), cur(2)),
            pl.BlockSpec((None, None, t, 2 * t), lambda bi, h, i: (layer, h, 0, 0)),
        ],
        out_specs=pl.BlockSpec((t, LANES), lambda bi, h, i: (bi * nq + i, h)),
        out_shape=jax.ShapeDtypeStruct((m, GROUP_W), BF16),
        compiler_params=_cparams("parallel", "parallel", "arbitrary"),
        name="band_prompt",
    )(qkv, qkv, qkv, qkv, qkv, table)


def _cached_head(c_ref, h):
    rows = c_ref.shape[0] // HEADS
    return c_ref[pl.ds(h, rows, stride=HEADS), :].astype(BF16)


def _head_cols(h):
    return slice(h * LANES, (h + 1) * LANES)


def _fox_sample_kernel(q_ref, kn_ref, vn_ref, kc_ref, vc_ref, bc_ref, bn_ref, o_ref, *, ts):
    rows = lax.broadcasted_iota(jnp.int32, (ts, ts), 0)
    cols = lax.broadcasted_iota(jnp.int32, (ts, ts), 1)
    for h in range(HEADS):
        c = _head_cols(h)
        q = q_ref[:, c]
        carry = _flash_update(q, _cached_head(kc_ref, h), _cached_head(vc_ref, h), bc_ref[h], None, _flash_init(ts))
        _, l, acc = _flash_update(q, kn_ref[:, c], vn_ref[:, c], bn_ref[h], cols <= rows, carry)
        o_ref[:, c] = (acc / l).astype(BF16)


def _band_sample_kernel(q_ref, kn_ref, vn_ref, kc_ref, vc_ref, bias_ref, o_ref, *, ts, lb):
    for h in range(HEADS):
        c = _head_cols(h)
        q = q_ref[:, c]
        carry = _flash_update(q, _cached_head(kc_ref, h), _cached_head(vc_ref, h), bias_ref[h, :, :lb], None,
                              _flash_init(ts))
        _, l, acc = _flash_update(q, kn_ref[:, c], vn_ref[:, c], bias_ref[h, :, lb:], None, carry)
        o_ref[:, c] = (acc / l).astype(BF16)


def _diff_sample_kernel(q_ref, kn_ref, vn_ref, kc_ref, vc_ref, lam_ref, g_ref, li_ref, o_ref, *, ts):
    for h in range(HEADS):
        c = _head_cols(h)
        q2 = _stack_diff_q(q_ref[:, c])
        carry = _flash_update(q2, _cached_head(kc_ref, h), _cached_head(vc_ref, h), None, None,
                              _flash_init(2 * ts))
        carry = _flash_update(q2, kn_ref[:, c], vn_ref[:, c], None, None, carry)
        o_ref[:, c] = _diff_finish(carry, ts, lam_ref, li_ref, g_ref).astype(BF16)


def _sb_sample_kernel(q_ref, kn_ref, vn_ref, kc_ref, vc_ref, tri_ref, o_ref, *, ts):
    tri2 = tri_ref[...]
    pad = jnp.zeros((LANES - ts, HEAD_DIM), BF16)
    rows = lax.broadcasted_iota(jnp.int32, (ts, LANES), 0)
    cols = lax.broadcasted_iota(jnp.int32, (ts, LANES), 1)
    for h in range(HEADS):
        c = _head_cols(h)
        q = q_ref[:, c]
        run, acc = _sb_span(q, jnp.concatenate([kn_ref[:, c], pad], axis=0),
                            jnp.concatenate([vn_ref[:, c], pad], axis=0), tri2, cols < rows,
                            jnp.zeros((ts, LANES), F32))
        _, pv = _sb_span(q, _cached_head(kc_ref, h), _cached_head(vc_ref, h), tri2, None, run)
        o_ref[:, c] = (acc + pv).astype(BF16)


def _sample_call(kern, qkv, group, cache_k, cache_v, layer, bs, ts, extra_in, extra_specs, name):
    m = qkv.shape[0]
    rows = cache_k.shape[1]
    new = lambda comp: pl.BlockSpec((ts, GROUP_W), lambda b: (b, group * 3 + comp))
    cache = pl.BlockSpec((None, rows, LANES), lambda b: (layer * bs + b, 0, 0))
    return pl.pallas_call(
        kern,
        grid=(bs,),
        in_specs=[new(0), new(1), new(2), cache, cache] + extra_specs,
        out_specs=pl.BlockSpec((ts, GROUP_W), lambda b: (b, 0)),
        out_shape=jax.ShapeDtypeStruct((m, GROUP_W), BF16),
        compiler_params=_cparams("parallel"),
        name=name,
    )(qkv, qkv, qkv, cache_k, cache_v, *extra_in)


def _rope_tables(pos):
    inv = ROPE_THETA ** (-jnp.arange(ROPE_HALF, dtype=F32) / ROPE_HALF)
    ang = pos.astype(F32)[:, None] * inv[None, :]
    cos, sin = jnp.cos(ang), jnp.sin(ang)
    return jnp.tile(cos, (1, 4)), jnp.concatenate([-sin, sin, -sin, sin], axis=1)


def _band_table(rel, nrows, ncols, q_off):
    period = nrows + ncols
    off = np.arange(period)
    off = np.where(off < ncols, off, off - period)
    vec = rel.astype(F32)[:, :, np.clip(q_off - off, -REL_CLIP, REL_CLIP) + REL_CLIP] * LOG2E
    toep = jnp.tile(vec, (1, 1, nrows))[:, :, :nrows * (period - 1)]
    toep = toep.reshape(rel.shape[0], rel.shape[1], nrows, period - 1)[:, :, :, :ncols]
    qc = (q_off + np.arange(nrows)[:, None]) // CHUNK
    kc = np.arange(ncols)[None, :] // CHUNK
    vis = (kc <= qc) & (kc >= qc - BAND_CHUNKS)
    return jnp.where(jnp.asarray(vis), toep, NEG_INF)


def _row_tile(m, want):
    t = min(m, want)
    assert m % t == 0
    return t


def kernel(x_prompt, x_sample, cache_a_k, cache_a_v, cache_a_logf, cache_b_k, cache_b_v, cache_c_k, cache_c_v, cache_d_k, cache_d_v, attn_norm_g, w_in, fox_b, band_rel_bias, diff_lambda, diff_norm_g, w_out, mlp_norm_g, w_up, w_down, final_norm_g):
    b, seq, d = x_prompt.shape
    bs, ts, _ = x_sample.shape
    depth = w_in.shape[0]
    past = cache_a_k.shape[2]
    lb = cache_b_k.shape[2]
    lbp = min(BAND_PAST, seq)
    mp, ms = b * seq, bs * ts
    assert w_in.shape[2] == QKV_W + HEADS and d == N_GROUPS * GROUP_W
    assert ts == CHUNK and past % LANES == 0 and past % CHUNK == 0 and lb == BAND_PAST

    tm_p = _row_tile(seq, 1024)
    tm_s = _row_tile(ms, 1024)
    assert tm_s % ts == 0 and tm_p >= lbp
    tm_res_p, tm_res_s = _row_tile(mp, 512), _row_tile(ms, 512)
    tm_up_p, tm_up_s = _row_tile(mp, 2048), _row_tile(ms, 2048)
    tn_up = _row_tile(w_up.shape[2], 1024)
    tm_down_p, tm_down_s = _row_tile(mp, 1024), _row_tile(ms, 1024)
    tk_down = _row_tile(w_up.shape[2], 512)
    t_attn = _row_tile(seq, 512)
    t_diff_k = _row_tile(seq, 1024)
    assert t_attn >= BAND_PAST and t_diff_k in (t_attn, 2 * t_attn)

    w_qkv = w_in[:, :, :QKV_W].astype(BF16)
    w_logit = jnp.pad(w_in[:, :, QKV_W:], ((0, 0), (0, 0), (0, LANES - HEADS))).astype(BF16)
    fox_b_pad = jnp.pad(fox_b.astype(F32), ((0, 0), (0, LANES - HEADS)))[:, None, :]
    w_out_b, w_up_b, w_down_b = w_out.astype(BF16), w_up.astype(BF16), w_down.astype(BF16)
    attn_g = attn_norm_g.astype(F32)[:, None, :]
    mlp_g = mlp_norm_g.astype(F32)[:, None, :]
    dng = diff_norm_g.astype(F32)[:, None, :]
    lam_rows = diff_lambda.astype(F32)
    cos_p, sin_p = _rope_tables(jnp.arange(seq))
    cos_s, sin_s = _rope_tables(past + jnp.arange(ts))
    cos_s, sin_s = jnp.tile(cos_s, (tm_s // ts, 1)), jnp.tile(sin_s, (tm_s // ts, 1))
    table_p = _band_table(band_rel_bias, t_attn, 2 * t_attn, t_attn)
    table_s = _band_table(band_rel_bias, ts, lb + ts, lb)
    tri2 = _tri2()

    flat = lambda c: c.reshape(depth * bs, c.shape[2] * HEADS, HEAD_DIM)
    ca_k, ca_v, cb_k, cb_v = flat(cache_a_k), flat(cache_a_v), flat(cache_b_k), flat(cache_b_v)
    cc_k, cc_v, cd_k, cd_v = flat(cache_c_k), flat(cache_c_v), flat(cache_d_k), flat(cache_d_v)

    next_g = jnp.concatenate([attn_g[1:], final_norm_g.astype(F32)[None, None, :]], axis=0)
    xp = x_prompt.reshape(mp, d)
    xs = x_sample.reshape(ms, d)
    hp = _norm(xp, attn_g, 0, tm_res_p)
    hs = _norm(xs, attn_g, 0, tm_res_s)
    p_st = s_st = None
    p_logf, s_logf = [], []
    for l in range(depth):
        lam_init = 0.8 - 0.6 * float(np.exp(-0.3 * l))
        li = jnp.asarray([lam_init], F32)

        h_dtype = BF16 if l + 1 < depth else F32
        qkv, logf, p_st = _proj(hp, w_qkv, w_logit, fox_b_pad, cos_p, sin_p, l, depth, tm_p, lbp, p_st)
        logf = logf[:, :HEADS].reshape(b, seq, HEADS)
        p_logf.append(logf)
        key_bias = jnp.cumsum(logf, axis=1) * (-LOG2E)
        oa = _fox_prompt(qkv, key_bias, b, seq, t_attn)
        ob = _band_prompt(qkv, table_p, l, b, seq, t_attn)
        oc = _diff_prompt(qkv, lam_rows, dng, li, l, b, seq, t_attn, t_diff_k)
        od = _sb_prompt(qkv, tri2, b, seq, t_attn)
        xp, hm = _out_proj(xp, [oa, ob, oc, od], w_out_b, mlp_g, l, tm_res_p)
        u = _up(hm, w_up_b, l, tm_up_p, tn_up)
        xp, hp = _down(xp, u, w_down_b, next_g, l, h_dtype, tm_down_p, tk_down)

        qkv, logf, s_st = _proj(hs, w_qkv, w_logit, fox_b_pad, cos_s, sin_s, l, depth, tm_s, 0, s_st)
        logf = logf[:, :HEADS].reshape(bs, ts, HEADS)
        s_logf.append(logf)
        cum_f = jnp.cumsum(jnp.concatenate([cache_a_logf[l].astype(F32), logf], axis=1), axis=1)
        kb = jnp.swapaxes(cum_f * (-LOG2E), 1, 2).reshape(bs, HEADS, 1, past + ts)
        kb_specs = [pl.BlockSpec((None, HEADS, 1, past), lambda bi: (bi, 0, 0, 0)),
                    pl.BlockSpec((None, HEADS, 1, ts), lambda bi: (bi, 0, 0, 0))]
        oa = _sample_call(functools.partial(_fox_sample_kernel, ts=ts), qkv, 0, ca_k, ca_v, l, bs, ts,
                          [kb[..., :past], kb[..., past:]], kb_specs, "fox_sample")
        ob = _sample_call(functools.partial(_band_sample_kernel, ts=ts, lb=lb), qkv, 1, cb_k, cb_v,
                          l, bs, ts, [table_s],
                          [pl.BlockSpec((None, HEADS, ts, lb + ts), lambda bi: (l, 0, 0, 0))], "band_sample")
        oc = _sample_call(functools.partial(_diff_sample_kernel, ts=ts), qkv, 2, cc_k, cc_v, l, bs, ts,
                          [lam_rows, dng, li],
                          [pl.BlockSpec((None, 4, DIFF_HALF), lambda bi: (l, 0, 0)),
                           pl.BlockSpec((None, 1, HEAD_DIM), lambda bi: (l, 0, 0)),
                           pl.BlockSpec(memory_space=pltpu.SMEM)], "diff_sample")
        od = _sample_call(functools.partial(_sb_sample_kernel, ts=ts), qkv, 3, cd_k, cd_v,
                          l, bs, ts, [tri2], [pl.BlockSpec((2 * LANES, 2 * LANES), lambda bi: (0, 0))],
                          "sb_sample")
        xs, hm = _out_proj(xs, [oa, ob, oc, od], w_out_b, mlp_g, l, tm_res_s)
        u = _up(hm, w_up_b, l, tm_up_s, tn_up)
        xs, hs = _down(xs, u, w_down_b, next_g, l, h_dtype, tm_down_s, tk_down)

    y_prompt = hp.reshape(b, seq, d)
    y_sample = hs.reshape(bs, ts, d)
    pst = lambda n, rows: p_st[n].reshape(depth, b, rows, HEADS, HEAD_DIM)
    sst = lambda n: s_st[n].reshape(depth, bs, ts, HEADS, HEAD_DIM)
    return (y_prompt, y_sample,
            pst(0, seq), pst(1, seq), jnp.stack(p_logf), pst(2, lbp), pst(3, lbp),
            pst(4, seq), pst(5, seq), pst(6, seq), pst(7, seq),
            sst(0), sst(1), jnp.stack(s_logf),
            jnp.concatenate([cache_b_k[:, :, ts:], sst(2)], axis=2),
            jnp.concatenate([cache_b_v[:, :, ts:], sst(3)], axis=2),
            sst(4), sst(5), sst(6), sst(7))
```
